```python
import jax, jax.numpy as jnp
from jax import lax
import numpy as np

D_MODEL = 1024
BATCH = 4
SEQ = 4096
DEPTH = 1
DEC_BATCH = 128
DEC_SEQ = 1
PAST_LEN = 8192
PAGE_SIZE = 128

MIX_WIDTH = D_MODEL
MLA_WIDTH = D_MODEL // 2
POOL_WIDTH = MIX_WIDTH - MLA_WIDTH
MLA_HEADS = 4
V_HEAD = MLA_WIDTH // MLA_HEADS
QK_NOPE = 128
QK_ROPE = 64
Q_LORA = 384
KV_LORA = 256
MLA_SCALE = (QK_NOPE + QK_ROPE) ** -0.5
Q_BLOCK = 128
POOL_WINDOWS = (2, 4, 8, 16)
POOL_GROUPS = len(POOL_WINDOWS)
POOL_GROUP_W = POOL_WIDTH // POOL_GROUPS
POOL_STATE = max(POOL_WINDOWS) - 1
N_MEM = 256
MEM_HEADS = 4
MEM_HEAD_DIM = D_MODEL // MEM_HEADS
MEM_SCALE = MEM_HEAD_DIM ** -0.5
D_FF = -(-8 * D_MODEL // (3 * 256)) * 256
IN_WIDTH = Q_LORA + KV_LORA + QK_ROPE + POOL_WIDTH
ROPE_THETA = 10000.0
EPS = 1e-6

kernel_name = "hymba_mla_pool_memxattn_step"


def rmsnorm(x, g):
    xf = x.astype(jnp.float32)
    y = xf * lax.rsqrt(jnp.mean(xf * xf, -1, keepdims=True) + EPS)
    return (y * g.astype(jnp.float32)).astype(x.dtype)


def rope_tables(pos):
    half = QK_ROPE // 2
    inv = jnp.power(ROPE_THETA, -(jnp.arange(half, dtype=jnp.float32) / half))
    ang = pos.astype(jnp.float32)[:, None] * inv[None, :]
    return jnp.cos(ang), jnp.sin(ang)


def apply_rope(x, cos, sin):
    xf = x.astype(jnp.float32)
    x1, x2 = xf[..., : QK_ROPE // 2], xf[..., QK_ROPE // 2:]
    return jnp.concatenate([x1 * cos - x2 * sin, x1 * sin + x2 * cos], -1).astype(x.dtype)


def mixer_inputs(h, pos, w_in, q_norm, w_uq, kv_norm, w_uk):
    B, T = h.shape[:2]
    u = h @ w_in
    i0, i1, i2 = Q_LORA, Q_LORA + KV_LORA, Q_LORA + KV_LORA + QK_ROPE
    cq, ckv, kr, up = u[..., :i0], u[..., i0:i1], u[..., i1:i2], u[..., i2:]
    q = (rmsnorm(cq, q_norm) @ w_uq).reshape(B, T, MLA_HEADS, QK_NOPE + QK_ROPE)
    cos, sin = rope_tables(pos)
    q_rope = apply_rope(q[..., QK_NOPE:], cos[:, None, :], sin[:, None, :])
    q_lat = jnp.einsum('bthn,rhn->bthr', q[..., :QK_NOPE], w_uk)
    c = rmsnorm(ckv, kv_norm)
    k_rope = apply_rope(kr, cos, sin)
    return q_lat, q_rope, c, k_rope, up


def mla_scores(q_lat, q_rope, c, k_rope):
    s = jnp.einsum('bthr,bsr->bhts', q_lat, c).astype(jnp.float32)
    s = s + jnp.einsum('bthp,bsp->bhts', q_rope, k_rope).astype(jnp.float32)
    return s * MLA_SCALE


def mla_attend_prompt(q_lat, q_rope, c, k_rope):
    B, T = q_lat.shape[:2]
    nb = T // Q_BLOCK
    kpos = jnp.arange(T)
    ql = q_lat.reshape(B, nb, Q_BLOCK, MLA_HEADS, KV_LORA).transpose(1, 0, 2, 3, 4)
    qr = q_rope.reshape(B, nb, Q_BLOCK, MLA_HEADS, QK_ROPE).transpose(1, 0, 2, 3, 4)
    qpos = jnp.arange(T).reshape(nb, Q_BLOCK)

    def block(args):
        qlb, qrb, qp = args
        s = mla_scores(qlb, qrb, c, k_rope)
        s = jnp.where(kpos[None, :] <= qp[:, None], s, -jnp.inf)
        p = jax.nn.softmax(s, -1).astype(c.dtype)
        return jnp.einsum('bhts,bsr->bthr', p, c)

    o = lax.map(block, (ql, qr, qpos))
    return o.transpose(1, 0, 2, 3, 4).reshape(B, T, MLA_HEADS, KV_LORA)


def mla_attend_sample(q_lat, q_rope, c_past, kr_past, c_new, kr_new):
    T = q_lat.shape[1]
    P = c_past.shape[1]
    s_past = mla_scores(q_lat, q_rope, c_past, kr_past)
    s_new = mla_scores(q_lat, q_rope, c_new, kr_new)
    s_new = jnp.where(jnp.tril(jnp.ones((T, T), bool)), s_new, -jnp.inf)
    p = jax.nn.softmax(jnp.concatenate([s_past, s_new], -1), -1).astype(c_new.dtype)
    return (jnp.einsum('bhts,bsr->bthr', p[..., :P], c_past)
            + jnp.einsum('bhts,bsr->bthr', p[..., P:], c_new))


def pool_mix(xp, prev, pos0, w_pool, pool_scale):
    B, T, C = xp.shape
    P = prev.shape[1]
    buf = jnp.concatenate([prev, xp], 1)
    cs = jnp.concatenate([jnp.zeros((B, 1, C), jnp.float32),
                          jnp.cumsum(buf.astype(jnp.float32), 1)], 1)
    pos = pos0 + jnp.arange(T)
    hi = cs[:, P + 1:P + 1 + T]
    xf = xp.astype(jnp.float32)
    outs = []
    for g, w in enumerate(POOL_WINDOWS):
        sl = slice(g * POOL_GROUP_W, (g + 1) * POOL_GROUP_W)
        lo = cs[:, P + 1 - w:P + 1 - w + T, sl]
        cnt = jnp.minimum(w, pos + 1).astype(jnp.float32)[None, :, None]
        outs.append((hi[..., sl] - lo) / cnt - xf[..., sl])
    m = jnp.stack(outs, 2).astype(xp.dtype)
    y = jnp.einsum('btgc,gcd->btgd', m, w_pool).reshape(B, T, C) * pool_scale
    return y, buf[:, -POOL_STATE:]


def mixer_out(o_lat, pool_y, w_uv, w_out):
    B, T = o_lat.shape[:2]
    o_mla = jnp.einsum('bthr,rhv->bthv', o_lat, w_uv).reshape(B, T, MLA_WIDTH)
    return jnp.concatenate([o_mla, pool_y], -1) @ w_out


def mem_kv(mem, mem_norm, w_mk, w_mv):
    B = mem.shape[0]
    m = rmsnorm(mem, mem_norm)
    return ((m @ w_mk).reshape(B, N_MEM, MEM_HEADS, MEM_HEAD_DIM),
            (m @ w_mv).reshape(B, N_MEM, MEM_HEADS, MEM_HEAD_DIM))


def mem_attend(h, mk, mv, w_xq, w_xo):
    B, T = h.shape[:2]
    q = (h @ w_xq).reshape(B, T, MEM_HEADS, MEM_HEAD_DIM)
    s = jnp.einsum('bthd,bmhd->bhtm', q, mk).astype(jnp.float32) * MEM_SCALE
    p = jax.nn.softmax(s, -1).astype(mv.dtype)
    o = jnp.einsum('bhtm,bmhd->bthd', p, mv).reshape(B, T, MEM_HEADS * MEM_HEAD_DIM)
    return o @ w_xo


def swiglu(h, w_gate, w_up, w_down):
    return (jax.nn.silu(h @ w_gate) * (h @ w_up)) @ w_down


def setup_inputs(seed: int = 0) -> dict:
    key = jax.random.key(seed)
    ks = iter(jax.random.split(key, 40))
    f32 = jnp.float32

    def nrm(shape, scale=1.0):
        return jax.random.normal(next(ks), shape, f32) * scale

    def gain(shape):
        return 1.0 + 0.05 * nrm(shape)

    n_pages = PAST_LEN // PAGE_SIZE
    n_used = DEC_BATCH * n_pages
    n_phys = n_used + max(1, n_used // 4)
    page_table = jax.random.permutation(next(ks), n_phys)[:n_used].reshape(DEC_BATCH, n_pages).astype(jnp.int32)
    L = DEPTH
    return {
        "x_prompt": nrm((BATCH, SEQ, D_MODEL)),
        "x_sample": nrm((DEC_BATCH, DEC_SEQ, D_MODEL)),
        "mem_prompt": nrm((BATCH, N_MEM, D_MODEL)),
        "cache_kv_latent": nrm((L, n_phys, PAGE_SIZE, KV_LORA)),
        "cache_k_rope": nrm((L, n_phys, PAGE_SIZE, QK_ROPE)),
        "cache_pool": nrm((L, DEC_BATCH, POOL_STATE, POOL_WIDTH)),
        "cache_mem_k": nrm((L, DEC_BATCH, N_MEM, MEM_HEADS, MEM_HEAD_DIM)),
        "cache_mem_v": nrm((L, DEC_BATCH, N_MEM, MEM_HEADS, MEM_HEAD_DIM)),
        "page_table": page_table,
        "norm_mix": gain((L, D_MODEL)),
        "w_in": nrm((L, D_MODEL, IN_WIDTH), D_MODEL ** -0.5),
        "q_norm": gain((L, Q_LORA)),
        "w_uq": nrm((L, Q_LORA, MLA_HEADS * (QK_NOPE + QK_ROPE)), Q_LORA ** -0.5),
        "kv_norm": gain((L, KV_LORA)),
        "w_uk": nrm((L, KV_LORA, MLA_HEADS, QK_NOPE), KV_LORA ** -0.5),
        "w_uv": nrm((L, KV_LORA, MLA_HEADS, V_HEAD), KV_LORA ** -0.5),
        "w_pool": nrm((L, POOL_GROUPS, POOL_GROUP_W, POOL_GROUP_W), POOL_GROUP_W ** -0.5),
        "pool_scale": gain((L, POOL_WIDTH)),
        "w_out": nrm((L, MIX_WIDTH, D_MODEL), MIX_WIDTH ** -0.5),
        "norm_x": gain((L, D_MODEL)),
        "mem_norm": gain((L, D_MODEL)),
        "w_xq": nrm((L, D_MODEL, MEM_HEADS * MEM_HEAD_DIM), D_MODEL ** -0.5),
        "w_mk": nrm((L, D_MODEL, MEM_HEADS * MEM_HEAD_DIM), D_MODEL ** -0.5),
        "w_mv": nrm((L, D_MODEL, MEM_HEADS * MEM_HEAD_DIM), D_MODEL ** -0.5),
        "w_xo": nrm((L, MEM_HEADS * MEM_HEAD_DIM, D_MODEL), D_MODEL ** -0.5),
        "norm_ffn": gain((L, D_MODEL)),
        "w_gate": nrm((L, D_MODEL, D_FF), D_MODEL ** -0.5),
        "w_up": nrm((L, D_MODEL, D_FF), D_MODEL ** -0.5),
        "w_down": nrm((L, D_FF, D_MODEL), D_FF ** -0.5),
        "norm_final": gain((D_MODEL,)),
    }


def reference(x_prompt, x_sample, mem_prompt, cache_kv_latent, cache_k_rope, cache_pool,
              cache_mem_k, cache_mem_v, page_table, norm_mix, w_in, q_norm, w_uq, kv_norm,
              w_uk, w_uv, w_pool, pool_scale, w_out, norm_x, mem_norm, w_xq, w_mk, w_mv,
              w_xo, norm_ffn, w_gate, w_up, w_down, norm_final):
    B, T = x_prompt.shape[:2]
    DB, TS = x_sample.shape[:2]
    past = page_table.shape[1] * PAGE_SIZE
    pos_p = jnp.arange(T)
    pos_s = past + jnp.arange(TS)
    xp, xs = x_prompt, x_sample
    kvl_p, kr_p, pool_p, mk_p, mv_p = [], [], [], [], []
    kvl_s, kr_s, pool_s = [], [], []
    for l in range(DEPTH):
        ql, qr, c, kr, up = mixer_inputs(rmsnorm(xp, norm_mix[l]), pos_p, w_in[l], q_norm[l],
                                         w_uq[l], kv_norm[l], w_uk[l])
        o_lat = mla_attend_prompt(ql, qr, c, kr)
        py, pst = pool_mix(up, jnp.zeros((B, POOL_STATE, POOL_WIDTH), up.dtype), 0,
                           w_pool[l], pool_scale[l])
        xp = xp + mixer_out(o_lat, py, w_uv[l], w_out[l])
        mk, mv = mem_kv(mem_prompt, mem_norm[l], w_mk[l], w_mv[l])
        xp = xp + mem_attend(rmsnorm(xp, norm_x[l]), mk, mv, w_xq[l], w_xo[l])
        xp = xp + swiglu(rmsnorm(xp, norm_ffn[l]), w_gate[l], w_up[l], w_down[l])
        kvl_p.append(c); kr_p.append(kr); pool_p.append(pst); mk_p.append(mk); mv_p.append(mv)
        ql, qr, c_n, kr_n, us = mixer_inputs(rmsnorm(xs, norm_mix[l]), pos_s, w_in[l], q_norm[l],
                                             w_uq[l], kv_norm[l], w_uk[l])
        c_past = cache_kv_latent[l][page_table].reshape(DB, past, KV_LORA)
        kr_past = cache_k_rope[l][page_table].reshape(DB, past, QK_ROPE)
        o_lat = mla_attend_sample(ql, qr, c_past, kr_past, c_n, kr_n)
        sy, sst = pool_mix(us, cache_pool[l], past, w_pool[l], pool_scale[l])
        xs = xs + mixer_out(o_lat, sy, w_uv[l], w_out[l])
        xs = xs + mem_attend(rmsnorm(xs, norm_x[l]), cache_mem_k[l], cache_mem_v[l], w_xq[l], w_xo[l])
        xs = xs + swiglu(rmsnorm(xs, norm_ffn[l]), w_gate[l], w_up[l], w_down[l])
        kvl_s.append(c_n); kr_s.append(kr_n); pool_s.append(sst)
    y_prompt = rmsnorm(xp, norm_final)
    y_sample = rmsnorm(xs, norm_final)
    return (y_prompt, y_sample,
            jnp.stack(kvl_p), jnp.stack(kr_p), jnp.stack(pool_p), jnp.stack(mk_p), jnp.stack(mv_p),
            jnp.stack(kvl_s), jnp.stack(kr_s), jnp.stack(pool_s))
```

```python
import functools

import jax
import jax.numpy as jnp
from jax import lax
from jax.experimental import pallas as pl
from jax.experimental.pallas import tpu as pltpu

F32 = jnp.float32
BF16 = jnp.bfloat16

EPS = 1e-6
ROPE_THETA = 10000.0
PAGE_SIZE = 128
POOL_WINDOWS = (2, 4, 8, 16)
POOL_STATE = max(POOL_WINDOWS) - 1
POOL_HALO = 16
LANES = 128
VMEM_LIMIT = 56 * 1024 * 1024


def _rms(x, g):
    return x * lax.rsqrt(jnp.mean(x * x, -1, keepdims=True) + EPS) * g


def _dot(a, b):
    return jnp.dot(a, b, preferred_element_type=F32)


def _dot_nt(a, b):
    return lax.dot_general(a, b, (((1,), (1,)), ((), ())), preferred_element_type=F32)


def _const_spec(shape):
    nd = len(shape)
    return pl.BlockSpec(shape, lambda *_: (0,) * nd, pipeline_mode=pl.Buffered(1))


def _whole_spec(shape):
    nd = len(shape)
    return pl.BlockSpec(shape, lambda *_: (0,) * nd)


def _params(n_axes):
    return pltpu.CompilerParams(dimension_semantics=("arbitrary",) * n_axes,
                                vmem_limit_bytes=VMEM_LIMIT)


def _mla_inputs(x, gmix, win, gq, wuq, gkv, wuk, cos, sin, scale, dims):
    q_lora, kv_lora, pool_w, heads, nope, rope = dims
    h = _rms(x, gmix).astype(BF16)
    u = _dot(h, win)
    i1, i2, i3 = q_lora, q_lora + kv_lora, q_lora + kv_lora + pool_w
    cq, ckv, up, krs = u[:, :i1], u[:, i1:i2], u[:, i2:i3], u[:, i3:]
    q = _dot(_rms(cq, gq).astype(BF16), wuq)
    n0 = heads * nope
    n1 = n0 + heads * rope
    q_rope = (q[:, n0:n1] * cos + q[:, n1:] * sin) * scale
    q_lat = [(_dot(q[:, hh * nope:(hh + 1) * nope].astype(BF16), wuk[hh]) * scale).astype(BF16)
             for hh in range(heads)]
    q_rope = [q_rope[:, hh * rope:(hh + 1) * rope].astype(BF16) for hh in range(heads)]
    c = _rms(ckv, gkv)
    k_rope = (krs * cos[:, :2 * rope] + pltpu.roll(krs, rope, 1) * sin[:, :2 * rope])[:, :rope]
    return q_lat, q_rope, c, k_rope, up


def _pool_project(sums, up, cnt, wpool, pscale):
    cg = wpool.shape[-1]
    ys = []
    for g in range(len(POOL_WINDOWS)):
        m = sums[g] / cnt[g] - up[:, g * cg:(g + 1) * cg]
        ys.append(_dot(m.astype(BF16), wpool[g]))
    return jnp.concatenate(ys, -1) * pscale


def _front_prompt_kernel(x_ref, gmix_ref, win_ref, gq_ref, wuq_ref, gkv_ref, wuk_ref, cos_ref, sin_ref,
                         wpool_ref, pscale_ref,
                         qlat_ref, qrope_ref, c_ref, kr_ref, cbf_ref, krbf_ref, py_ref, pst_ref,
                         prev_ref, *, scale, dims, tm):
    t = pl.program_id(1)
    heads = dims[3]
    q_lat, q_rope, c, k_rope, up = _mla_inputs(
        x_ref[0], gmix_ref[...], win_ref[...], gq_ref[...], wuq_ref[...], gkv_ref[...], wuk_ref,
        cos_ref[...], sin_ref[...], scale, dims)
    for hh in range(heads):
        qlat_ref[0, hh] = q_lat[hh]
        qrope_ref[0, hh] = q_rope[hh]
    c_ref[0] = c
    cbf_ref[0] = c.astype(BF16)
    kr_ref[0] = k_rope
    krbf_ref[0] = k_rope.astype(BF16)

    @pl.when(t == 0)
    def _():
        prev_ref[...] = jnp.zeros_like(prev_ref)

    cg = wpool_ref.shape[-1]
    e = jnp.concatenate([prev_ref[...], up], 0)
    tail = up[tm - POOL_HALO:, :]
    prev_ref[...] = tail
    pst_ref[0] = tail
    sums = []
    s = e
    for g, w in enumerate(POOL_WINDOWS):
        s = s[:, (cg if g else 0):]
        s = s + pltpu.roll(s, w // 2, 0)
        sums.append(s[POOL_HALO:, :cg])
    pos1 = t * tm + lax.broadcasted_iota(jnp.int32, (tm, cg), 0) + 1
    cnt = [jnp.minimum(w, pos1).astype(F32) for w in POOL_WINDOWS]
    py_ref[0] = _pool_project(sums, up, cnt, wpool_ref, pscale_ref[...]).astype(BF16)


def _front_sample_kernel(x_ref, gmix_ref, win_ref, gq_ref, wuq_ref, gkv_ref, wuk_ref, cos_ref, sin_ref,
                         wpool_ref, pscale_ref, prev_ref,
                         qlat_ref, qrope_ref, c_ref, kr_ref, py_ref, up_ref, *, scale, dims, past):
    heads = dims[3]
    q_lat, q_rope, c, k_rope, up = _mla_inputs(
        x_ref[...], gmix_ref[...], win_ref[...], gq_ref[...], wuq_ref[...], gkv_ref[...], wuk_ref,
        cos_ref[...], sin_ref[...], scale, dims)
    for hh in range(heads):
        qlat_ref[hh] = q_lat[hh]
        qrope_ref[hh] = q_rope[hh]
    c_ref[...] = c
    kr_ref[...] = k_rope
    up_ref[...] = up

    cg = wpool_ref.shape[-1]
    sums, cnt = [], []
    for g, w in enumerate(POOL_WINDOWS):
        s = up[:, g * cg:(g + 1) * cg]
        for j in range(1, w):
            s = s + prev_ref[POOL_STATE - j][:, g * cg:(g + 1) * cg]
        sums.append(s)
        cnt.append(float(min(w, past + 1)))
    py_ref[...] = _pool_project(sums, up, cnt, wpool_ref, pscale_ref[...]).astype(BF16)


def _mix_out(o_lat_heads, py, x, wuv_ref, wout_ref):
    o_mla = jnp.concatenate(
        [_dot(o.astype(BF16), wuv_ref[hh]) for hh, o in enumerate(o_lat_heads)], -1).astype(BF16)
    wm = o_mla.shape[-1]
    return x + _dot(o_mla, wout_ref[:wm, :]) + _dot(py, wout_ref[wm:, :])


def _attn_prompt_kernel(ql_ref, qr_ref, c_ref, kr_ref, x_ref, py_ref, wuv_ref, wout_ref, o_ref,
                        m_ref, l_ref, acc_ref, *, tq, tk, heads):
    i = pl.program_id(1)
    rows = heads * tq
    ql = ql_ref[0].reshape(rows, ql_ref.shape[-1])
    qr = qr_ref[0].reshape(rows, qr_ref.shape[-1])
    m_ref[...] = jnp.full_like(m_ref, -jnp.inf)
    l_ref[...] = jnp.zeros_like(l_ref)
    acc_ref[...] = jnp.zeros_like(acc_ref)

    def step(j, masked):
        off = pl.multiple_of(j * tk, tk)
        cb = c_ref[0, pl.ds(off, tk), :]
        kb = kr_ref[0, pl.ds(off, tk), :]
        s = _dot_nt(ql, cb) + _dot_nt(qr, kb)
        if masked:
            qpos = i * tq + (lax.broadcasted_iota(jnp.int32, s.shape, 0) & (tq - 1))
            kpos = off + lax.broadcasted_iota(jnp.int32, s.shape, 1)
            s = jnp.where(kpos <= qpos, s, -jnp.inf)
        m_prev = m_ref[...]
        m_new = jnp.maximum(m_prev, jnp.max(s, -1, keepdims=True))
        alpha = jnp.exp(m_prev - m_new)
        p = jnp.exp(s - m_new)
        l_ref[...] = alpha * l_ref[...] + jnp.sum(p, -1, keepdims=True)
        acc_ref[...] = alpha * acc_ref[...] + _dot(p.astype(BF16), cb)
        m_ref[...] = m_new

    n_full = (i * tq) // tk

    def body(j, carry):
        step(j, False)
        return carry

    lax.fori_loop(0, n_full, body, 0)
    step(n_full, True)

    o = acc_ref[...] / l_ref[...]
    o_ref[0] = _mix_out([o[hh * tq:(hh + 1) * tq] for hh in range(heads)], py_ref[0], x_ref[0],
                        wuv_ref, wout_ref)


def _attn_paged_kernel(pt_ref, ql_ref, qr_ref, cn_ref, krn_ref, cache_c, cache_r, o_ref,
                       kbuf, rbuf, sem, *, n_pages):
    b = pl.program_id(0)
    nb = pl.num_programs(0)
    slot = b % 2

    def page_copies(bb, sl):
        out = []
        for p in range(n_pages):
            page = pt_ref[bb * n_pages + p]
            rows = pl.ds(p * PAGE_SIZE, PAGE_SIZE)
            out.append(pltpu.make_async_copy(cache_c.at[page], kbuf.at[sl, rows, :], sem.at[0, sl]))
            out.append(pltpu.make_async_copy(cache_r.at[page], rbuf.at[sl, rows, :], sem.at[1, sl]))
        return out

    @pl.when(b == 0)
    def _():
        for cp in page_copies(0, 0):
            cp.start()

    @pl.when(b + 1 < nb)
    def _():
        for cp in page_copies(b + 1, 1 - slot):
            cp.start()

    for cp in page_copies(b, slot):
        cp.wait()

    ql = ql_ref[0]
    qr = qr_ref[0]
    cb = kbuf[slot].astype(BF16)
    kb = rbuf[slot].astype(BF16)
    cn = cn_ref[0].astype(BF16).astype(F32)
    krn = krn_ref[0].astype(BF16).astype(F32)
    s = _dot_nt(ql, cb) + _dot_nt(qr, kb)
    s_new = (jnp.sum(ql.astype(F32) * cn, -1, keepdims=True)
             + jnp.sum(qr.astype(F32) * krn, -1, keepdims=True))
    m = jnp.maximum(jnp.max(s, -1, keepdims=True), s_new)
    p = jnp.exp(s - m)
    p_new = jnp.exp(s_new - m)
    l = jnp.sum(p, -1, keepdims=True) + p_new
    acc = _dot(p.astype(BF16), cb) + p_new.astype(BF16).astype(F32) * cn
    o_ref[0] = acc / l


def _mix_out_kernel(ol_ref, py_ref, x_ref, wuv_ref, wout_ref, o_ref, *, heads):
    o_ref[...] = _mix_out([ol_ref[hh] for hh in range(heads)], py_ref[...], x_ref[...],
                          wuv_ref, wout_ref)


def _mem_kv_kernel(mem_ref, g_ref, wk_ref, wv_ref, k_ref, v_ref, kbf_ref, vbf_ref):
    m = _rms(mem_ref[0], g_ref[...]).astype(BF16)
    k = _dot(m, wk_ref[...])
    v = _dot(m, wv_ref[...])
    k_ref[0] = k
    v_ref[0] = v
    kbf_ref[0] = k.astype(BF16)
    vbf_ref[0] = v.astype(BF16)


def _mem_attn_prompt_kernel(x_ref, g_ref, wq_ref, k_ref, v_ref, wo_ref, o_ref, *, heads, scale):
    x = x_ref[0]
    q = (_dot(_rms(x, g_ref[...]).astype(BF16), wq_ref[...]) * scale).astype(BF16)
    hd = q.shape[-1] // heads
    outs = []
    for hh in range(heads):
        sl = slice(hh * hd, (hh + 1) * hd)
        s = _dot_nt(q[:, sl], k_ref[0, :, sl])
        p = jnp.exp(s - jnp.max(s, -1, keepdims=True))
        o = _dot(p.astype(BF16), v_ref[0, :, sl]) / jnp.sum(p, -1, keepdims=True)
        outs.append(o.astype(BF16))
    o_ref[0] = x + _dot(jnp.concatenate(outs, -1), wo_ref[...])


def _mem_attn_sample_kernel(x_ref, g_ref, wq_ref, k_ref, v_ref, wo_ref, o_ref, q_ref, a_ref,
                            *, heads, scale, nb):
    g = pl.program_id(0)

    @pl.when(g == 0)
    def _():
        q = _dot(_rms(x_ref[...], g_ref[...]).astype(BF16), wq_ref[...]) * scale
        q_ref[...] = q.astype(BF16).astype(F32)

    hd = q_ref.shape[-1] // heads
    r0 = pl.multiple_of(g * nb, nb)
    qs = q_ref[pl.ds(r0, nb), :]
    rows = []
    for bb in range(nb):
        kq = k_ref[bb].astype(BF16).astype(F32) * qs[bb:bb + 1, :]
        v = v_ref[bb].astype(BF16).astype(F32)
        outs = []
        for hh in range(heads):
            sl = slice(hh * hd, (hh + 1) * hd)
            s = jnp.sum(kq[:, sl], -1, keepdims=True)
            p = jnp.exp(s - jnp.max(s, 0, keepdims=True))
            l = jnp.sum(p, 0, keepdims=True)
            pv = jnp.sum(p.astype(BF16).astype(F32) * v[:, sl], 0, keepdims=True)
            outs.append(pv / l)
        rows.append(jnp.concatenate(outs, -1))
    a_ref[pl.ds(r0, nb), :] = jnp.concatenate(rows, 0)

    @pl.when(g == pl.num_programs(0) - 1)
    def _():
        o_ref[...] = x_ref[...] + _dot(a_ref[...].astype(BF16), wo_ref[...])


def _ffn_kernel(x_ref, g_ref, wg_ref, wu_ref, wd_ref, gf_ref, o_ref):
    x = x_ref[0]
    h = _rms(x, g_ref[...]).astype(BF16)
    gate = _dot(h, wg_ref[...])
    a = (gate * jax.nn.sigmoid(gate) * _dot(h, wu_ref[...])).astype(BF16)
    o_ref[0] = _rms(x + _dot(a, wd_ref[...]), gf_ref[...])


def _rope_tables(pos, rope, heads):
    half = rope // 2
    inv = jnp.power(ROPE_THETA, -(jnp.arange(half, dtype=F32) / half))
    ang = pos.astype(F32)[:, None] * inv[None, :]
    cos, sin = jnp.cos(ang), jnp.sin(ang)
    return (jnp.tile(jnp.concatenate([cos, cos], -1), (1, heads)),
            jnp.tile(jnp.concatenate([-sin, sin], -1), (1, heads)))


def _swap_halves(w):
    half = w.shape[-1] // 2
    return jnp.concatenate([w[..., half:], w[..., :half]], -1)


def _row(v):
    return v.reshape(1, -1)


def _tile(n, pref):
    return pref if n % pref == 0 else n


def _ffn(x, g, wg, wu, wd, gf, tm):
    bsz, t, d = x.shape
    spec = pl.BlockSpec((1, tm, d), lambda b, i: (b, i, 0))
    return pl.pallas_call(
        _ffn_kernel,
        grid=(bsz, t // tm),
        in_specs=[spec, _const_spec(g.shape), _const_spec(wg.shape), _const_spec(wu.shape),
                  _const_spec(wd.shape), _const_spec(gf.shape)],
        out_specs=spec,
        out_shape=jax.ShapeDtypeStruct(x.shape, F32),
        compiler_params=_params(2),
        name="ffn",
    )(x, g, wg, wu, wd, gf)


def kernel(x_prompt, x_sample, mem_prompt, cache_kv_latent, cache_k_rope, cache_pool, cache_mem_k, cache_mem_v, page_table, norm_mix, w_in, q_norm, w_uq, kv_norm, w_uk, w_uv, w_pool, pool_scale, w_out, norm_x, mem_norm, w_xq, w_mk, w_mv, w_xo, norm_ffn, w_gate, w_up, w_down, norm_final):
    depth = w_in.shape[0]
    assert depth == 1, "single-layer trunk"
    bsz, seq, d = x_prompt.shape
    db, ts, _ = x_sample.shape
    assert ts == 1, "one new token per request"
    q_lora, kv_lora = q_norm.shape[1], kv_norm.shape[1]
    heads, nope = w_uk.shape[2], w_uk.shape[3]
    rope = cache_k_rope.shape[-1]
    pool_w = pool_scale.shape[1]
    v_head = w_uv.shape[3]
    n_mem, mem_heads, mem_hd = cache_mem_k.shape[2], cache_mem_k.shape[3], cache_mem_k.shape[4]
    n_pages = page_table.shape[1]
    assert cache_kv_latent.shape[2] == PAGE_SIZE
    past = n_pages * PAGE_SIZE
    mla_scale = float((nope + rope) ** -0.5)
    mem_scale = float(mem_hd ** -0.5)
    dims = (q_lora, kv_lora, pool_w, heads, nope, rope)
    l = 0

    i1, i2, i3 = q_lora, q_lora + kv_lora, q_lora + kv_lora + rope
    wi = w_in[l]
    w_kr = wi[:, i2:i3]
    win = jnp.concatenate([wi[:, :i2], wi[:, i3:], w_kr, _swap_halves(w_kr)], -1).astype(BF16)
    wq3 = w_uq[l].reshape(q_lora, heads, nope + rope)
    wq_rope = wq3[:, :, nope:]
    wuq = jnp.concatenate([wq3[:, :, :nope].reshape(q_lora, heads * nope),
                           wq_rope.reshape(q_lora, heads * rope),
                           _swap_halves(wq_rope).reshape(q_lora, heads * rope)], -1).astype(BF16)
    wuk = jnp.transpose(w_uk[l], (1, 2, 0)).astype(BF16)
    wuv = jnp.transpose(w_uv[l], (1, 0, 2)).astype(BF16)
    wpool = w_pool[l].astype(BF16)
    wout = w_out[l].astype(BF16)
    wxq, wxo = w_xq[l].astype(BF16), w_xo[l].astype(BF16)
    wmk, wmv = w_mk[l].astype(BF16), w_mv[l].astype(BF16)
    wg, wu, wd = w_gate[l].astype(BF16), w_up[l].astype(BF16), w_down[l].astype(BF16)
    gmix, gq, gkv = _row(norm_mix[l]), _row(q_norm[l]), _row(kv_norm[l])
    gx, gmem, gffn, gfin = _row(norm_x[l]), _row(mem_norm[l]), _row(norm_ffn[l]), _row(norm_final)
    pscale = _row(pool_scale[l])
    front_consts = (gmix, win, gq, wuq, gkv, wuk)
    hr = heads * rope

    tm = _tile(seq, 512)
    cos_p, sin_p = _rope_tables(jnp.arange(seq), rope, heads)
    row_spec = lambda w: pl.BlockSpec((1, tm, w), lambda b, t: (b, t, 0))
    head_spec = lambda w: pl.BlockSpec((1, heads, tm, w), lambda b, t: (b, 0, t, 0))
    tab_spec = pl.BlockSpec((tm, hr), lambda b, t: (t, 0))
    qlat_p, qrope_p, c_p, kr_p, cbf_p, krbf_p, py_p, pst_p = pl.pallas_call(
        functools.partial(_front_prompt_kernel, scale=mla_scale, dims=dims, tm=tm),
        grid=(bsz, seq // tm),
        in_specs=[row_spec(d)] + [_const_spec(a.shape) for a in front_consts]
        + [tab_spec, tab_spec, _const_spec(wpool.shape), _const_spec(pscale.shape)],
        out_specs=[head_spec(kv_lora), head_spec(rope), row_spec(kv_lora), row_spec(rope),
                   row_spec(kv_lora), row_spec(rope), row_spec(pool_w),
                   pl.BlockSpec((1, POOL_HALO, pool_w), lambda b, t: (b, 0, 0))],
        out_shape=[jax.ShapeDtypeStruct((bsz, heads, seq, kv_lora), BF16),
                   jax.ShapeDtypeStruct((bsz, heads, seq, rope), BF16),
                   jax.ShapeDtypeStruct((bsz, seq, kv_lora), F32),
                   jax.ShapeDtypeStruct((bsz, seq, rope), F32),
                   jax.ShapeDtypeStruct((bsz, seq, kv_lora), BF16),
                   jax.ShapeDtypeStruct((bsz, seq, rope), BF16),
                   jax.ShapeDtypeStruct((bsz, seq, pool_w), BF16),
                   jax.ShapeDtypeStruct((bsz, POOL_HALO, pool_w), F32)],
        scratch_shapes=[pltpu.VMEM((POOL_HALO, pool_w), F32)],
        compiler_params=_params(2),
        name="front_prompt",
    )(x_prompt, *front_consts, cos_p, sin_p, wpool, pscale)

    tq = _tile(seq, 128)
    tk = _tile(seq, 512)
    q_spec = lambda w: pl.BlockSpec((1, heads, tq, w), lambda b, i: (b, 0, i, 0))
    kv_spec = lambda w: pl.BlockSpec((1, seq, w), lambda b, i: (b, 0, 0))
    tq_spec = lambda w: pl.BlockSpec((1, tq, w), lambda b, i: (b, i, 0))
    x1_p = pl.pallas_call(
        functools.partial(_attn_prompt_kernel, tq=tq, tk=tk, heads=heads),
        grid=(bsz, seq // tq),
        in_specs=[q_spec(kv_lora), q_spec(rope), kv_spec(kv_lora), kv_spec(rope), tq_spec(d),
                  tq_spec(pool_w), _const_spec(wuv.shape), _const_spec(wout.shape)],
        out_specs=tq_spec(d),
        out_shape=jax.ShapeDtypeStruct((bsz, seq, d), F32),
        scratch_shapes=[pltpu.VMEM((heads * tq, 1), F32), pltpu.VMEM((heads * tq, 1), F32),
                        pltpu.VMEM((heads * tq, kv_lora), F32)],
        compiler_params=_params(2),
        name="attn_prompt",
    )(qlat_p, qrope_p, cbf_p, krbf_p, x_prompt, py_p, wuv, wout)

    mem_spec = pl.BlockSpec((1, n_mem, d), lambda b: (b, 0, 0))
    mk_p, mv_p, mkbf_p, mvbf_p = pl.pallas_call(
        _mem_kv_kernel,
        grid=(bsz,),
        in_specs=[mem_spec, _const_spec(gmem.shape), _const_spec(wmk.shape), _const_spec(wmv.shape)],
        out_specs=[mem_spec] * 4,
        out_shape=[jax.ShapeDtypeStruct((bsz, n_mem, d), F32)] * 2
        + [jax.ShapeDtypeStruct((bsz, n_mem, d), BF16)] * 2,
        compiler_params=_params(1),
        name="mem_kv",
    )(mem_prompt, gmem, wmk, wmv)

    tmem = _tile(seq, 512)
    xm_spec = pl.BlockSpec((1, tmem, d), lambda b, i: (b, i, 0))
    memkv_spec = pl.BlockSpec((1, n_mem, d), lambda b, i: (b, 0, 0))
    x2_p = pl.pallas_call(
        functools.partial(_mem_attn_prompt_kernel, heads=mem_heads, scale=mem_scale),
        grid=(bsz, seq // tmem),
        in_specs=[xm_spec, _const_spec(gx.shape), _const_spec(wxq.shape), memkv_spec, memkv_spec,
                  _const_spec(wxo.shape)],
        out_specs=xm_spec,
        out_shape=jax.ShapeDtypeStruct((bsz, seq, d), F32),
        compiler_params=_params(2),
        name="mem_attn_prompt",
    )(x1_p, gx, wxq, mkbf_p, mvbf_p, wxo)

    y_prompt = _ffn(x2_p, gffn, wg, wu, wd, gfin, _tile(seq, 256))

    xs = x_sample.reshape(db, d)
    cos_s, sin_s = _rope_tables(jnp.full((db,), past), rope, heads)
    prev_s = jnp.transpose(cache_pool[l], (1, 0, 2))
    sample_in = (xs, *front_consts, cos_s, sin_s, wpool, pscale, prev_s)
    qlat_s, qrope_s, c_s, kr_s, py_s, up_s = pl.pallas_call(
        functools.partial(_front_sample_kernel, scale=mla_scale, dims=dims, past=past),
        grid=(1,),
        in_specs=[_const_spec(a.shape) for a in sample_in],
        out_specs=[_whole_spec(s) for s in ((heads, db, kv_lora), (heads, db, rope), (db, kv_lora),
                                            (db, rope), (db, pool_w), (db, pool_w))],
        out_shape=[jax.ShapeDtypeStruct((heads, db, kv_lora), BF16),
                   jax.ShapeDtypeStruct((heads, db, rope), BF16),
                   jax.ShapeDtypeStruct((db, kv_lora), F32),
                   jax.ShapeDtypeStruct((db, rope), F32),
                   jax.ShapeDtypeStruct((db, pool_w), BF16),
                   jax.ShapeDtypeStruct((db, pool_w), F32)],
        compiler_params=_params(1),
        name="front_sample",
    )(*sample_in)

    b_spec = lambda h, w: pl.BlockSpec((1, h, w), lambda b, pt: (b, 0, 0))
    olat_s = pl.pallas_call(
        functools.partial(_attn_paged_kernel, n_pages=n_pages),
        grid_spec=pltpu.PrefetchScalarGridSpec(
            num_scalar_prefetch=1,
            grid=(db,),
            in_specs=[b_spec(heads, kv_lora), b_spec(heads, rope), b_spec(1, kv_lora), b_spec(1, rope),
                      pl.BlockSpec(memory_space=pl.ANY), pl.BlockSpec(memory_space=pl.ANY)],
            out_specs=b_spec(heads, kv_lora),
            scratch_shapes=[pltpu.VMEM((2, past, kv_lora), F32), pltpu.VMEM((2, past, rope), F32),
                            pltpu.SemaphoreType.DMA((2, 2))],
        ),
        out_shape=jax.ShapeDtypeStruct((db, heads, kv_lora), F32),
        compiler_params=_params(1),
        name="attn_paged",
    )(page_table.reshape(-1), jnp.swapaxes(qlat_s, 0, 1), jnp.swapaxes(qrope_s, 0, 1), c_s.reshape(db, 1, kv_lora), kr_s.reshape(db, 1, rope),
      cache_kv_latent[l], cache_k_rope[l])

    mix_in = (jnp.swapaxes(olat_s, 0, 1), py_s, xs, wuv, wout)
    x1_s = pl.pallas_call(
        functools.partial(_mix_out_kernel, heads=heads),
        grid=(1,),
        in_specs=[_const_spec(a.shape) for a in mix_in],
        out_specs=_whole_spec((db, d)),
        out_shape=jax.ShapeDtypeStruct((db, d), F32),
        compiler_params=_params(1),
        name="mix_out_sample",
    )(*mix_in)

    nb = 8
    assert db % nb == 0
    cache_spec = pl.BlockSpec((nb, n_mem, mem_heads * mem_hd), lambda g: (g, 0, 0))
    x2_s = pl.pallas_call(
        functools.partial(_mem_attn_sample_kernel, heads=mem_heads, scale=mem_scale, nb=nb),
        grid=(db // nb,),
        in_specs=[_const_spec((db, d)), _const_spec(gx.shape), _const_spec(wxq.shape), cache_spec,
                  cache_spec, _const_spec(wxo.shape)],
        out_specs=_whole_spec((db, d)),
        out_shape=jax.ShapeDtypeStruct((db, d), F32),
        scratch_shapes=[pltpu.VMEM((db, d), F32), pltpu.VMEM((db, d), F32)],
        compiler_params=_params(1),
        name="mem_attn_sample",
    )(x1_s, gx, wxq, cache_mem_k[l].reshape(db, n_mem, mem_heads * mem_hd),
      cache_mem_v[l].reshape(db, n_mem, mem_heads * mem_hd), wxo)

    y_sample = _ffn(x2_s.reshape(1, db, d), gffn, wg, wu, wd, gfin, db).reshape(db, ts, d)

    mem_shape = (depth, bsz, n_mem, mem_heads, mem_hd)
    return (y_prompt, y_sample,
            c_p[None], kr_p[None], pst_p[None, :, POOL_HALO - POOL_STATE:],
            mk_p.reshape(mem_shape), mv_p.reshape(mem_shape),
            c_s.reshape(depth, db, ts, kv_lora), kr_s.reshape(depth, db, ts, rope),
            jnp.concatenate([cache_pool[l][:, 1:], up_s[:, None]], 1)[None])
```

```python
import functools

import jax
import jax.numpy as jnp
from jax import lax
from jax.experimental import pallas as pl
from jax.experimental.pallas import tpu as pltpu

F32 = jnp.float32
BF16 = jnp.bfloat16

EPS = 1e-6
ROPE_THETA = 10000.0
PAGE_SIZE = 128
POOL_WINDOWS = (2, 4, 8, 16)
POOL_STATE = max(POOL_WINDOWS) - 1
POOL_HALO = 16
LANES = 128
VMEM_LIMIT = 56 * 1024 * 1024
ATTN_ROWS = 128
ATTN_AHEAD = 6


def _rms(x, g):
    return x * lax.rsqrt(jnp.mean(x * x, -1, keepdims=True) + EPS) * g


def _dot(a, b):
    return jnp.dot(a, b, preferred_element_type=F32)


def _dot_nt(a, b):
    return lax.dot_general(a, b, (((1,), (1,)), ((), ())), preferred_element_type=F32)


def _lane_tile(x, n):
    return jnp.concatenate([x] * n, -1)


def _const_spec(shape):
    nd = len(shape)
    return pl.BlockSpec(shape, lambda *_: (0,) * nd, pipeline_mode=pl.Buffered(1))


def _whole_spec(shape):
    nd = len(shape)
    return pl.BlockSpec(shape, lambda *_: (0,) * nd)


def _params(n_axes):
    return pltpu.CompilerParams(dimension_semantics=("arbitrary",) * n_axes,
                                vmem_limit_bytes=VMEM_LIMIT)


def _mla_inputs(x, gmix, win, gq, wuq, gkv, wuk, cos, sin, scale, dims):
    q_lora, kv_lora, pool_w, heads, nope, rope = dims
    h = _rms(x, gmix).astype(BF16)
    u = _dot(h, win)
    i1, i2, i3 = q_lora, q_lora + kv_lora, q_lora + kv_lora + pool_w
    cq, ckv, up, krs = u[:, :i1], u[:, i1:i2], u[:, i2:i3], u[:, i3:]
    q = _dot(_rms(cq, gq).astype(BF16), wuq)
    n0 = heads * nope
    n1 = n0 + heads * rope
    q_rope = (q[:, n0:n1] * cos + q[:, n1:] * sin) * scale
    q_lat = [(_dot(q[:, hh * nope:(hh + 1) * nope].astype(BF16), wuk[hh]) * scale).astype(BF16)
             for hh in range(heads)]
    q_rope = [q_rope[:, hh * rope:(hh + 1) * rope].astype(BF16) for hh in range(heads)]
    c = _rms(ckv, gkv)
    k_rope = krs * cos[:, :2 * rope] + pltpu.roll(krs, rope, 1) * sin[:, :2 * rope]
    return q_lat, q_rope, c, k_rope, up


def _pool_project(sums, up, cnt, wpool, pscale):
    cg = wpool.shape[-1]
    ys = []
    for g in range(len(POOL_WINDOWS)):
        m = sums[g] / cnt[g] - up[:, g * cg:(g + 1) * cg]
        ys.append(_dot(m.astype(BF16), wpool[g]))
    return jnp.concatenate(ys, -1) * pscale


def _front_prompt_kernel(x_ref, gmix_ref, win_ref, gq_ref, wuq_ref, gkv_ref, wuk_ref, cos_ref, sin_ref,
                         wpool_ref, pscale_ref,
                         qlat_ref, qrope_ref, c_ref, krt_ref, cbf_ref, ctbf_ref, krtbf_ref, py_ref, pst_ref,
                         prev_ref, *, scale, dims, tm):
    t = pl.program_id(1)
    heads, rope = dims[3], dims[5]
    q_lat, q_rope, c, k_rope, up = _mla_inputs(
        x_ref[0], gmix_ref[...], win_ref[...], gq_ref[...], wuq_ref[...], gkv_ref[...], wuk_ref,
        cos_ref[...], sin_ref[...], scale, dims)
    for hh in range(heads):
        qlat_ref[0, hh] = q_lat[hh]
        qrope_ref[0, hh] = q_rope[hh]
    c_ref[0] = c
    cbf_ref[0] = c.astype(BF16)
    ctbf_ref[0] = c.T.astype(BF16)
    krt = k_rope.T[:rope]
    krt_ref[0] = krt
    krtbf_ref[0] = krt.astype(BF16)

    @pl.when(t == 0)
    def _():
        prev_ref[...] = jnp.zeros_like(prev_ref)

    cg = wpool_ref.shape[-1]
    e = jnp.concatenate([prev_ref[...], up], 0)
    tail = up[tm - POOL_HALO:, :]
    prev_ref[...] = tail
    pst_ref[0] = tail
    sums = []
    s = e
    for g, w in enumerate(POOL_WINDOWS):
        s = s[:, (cg if g else 0):]
        s = s + pltpu.roll(s, w // 2, 0)
        sums.append(s[POOL_HALO:, :cg])
    pos1 = t * tm + lax.broadcasted_iota(jnp.int32, (tm, cg), 0) + 1
    cnt = [jnp.minimum(w, pos1).astype(F32) for w in POOL_WINDOWS]
    py_ref[0] = _pool_project(sums, up, cnt, wpool_ref, pscale_ref[...]).astype(BF16)


def _front_sample_kernel(x_ref, gmix_ref, win_ref, gq_ref, wuq_ref, gkv_ref, wuk_ref, cos_ref, sin_ref,
                         wpool_ref, pscale_ref, prev_ref,
                         qlat_ref, qrope_ref, c_ref, kr_ref, py_ref, up_ref, *, scale, dims, past):
    heads, rope = dims[3], dims[5]
    q_lat, q_rope, c, k_rope, up = _mla_inputs(
        x_ref[...], gmix_ref[...], win_ref[...], gq_ref[...], wuq_ref[...], gkv_ref[...], wuk_ref,
        cos_ref[...], sin_ref[...], scale, dims)
    for hh in range(heads):
        qlat_ref[hh] = q_lat[hh]
        qrope_ref[hh] = q_rope[hh]
    c_ref[...] = c
    kr_ref[...] = k_rope[:, :rope]
    up_ref[...] = up

    cg = wpool_ref.shape[-1]
    sums, cnt = [], []
    for g, w in enumerate(POOL_WINDOWS):
        s = up[:, g * cg:(g + 1) * cg]
        for j in range(1, w):
            s = s + prev_ref[POOL_STATE - j][:, g * cg:(g + 1) * cg]
        sums.append(s)
        cnt.append(float(min(w, past + 1)))
    py_ref[...] = _pool_project(sums, up, cnt, wpool_ref, pscale_ref[...]).astype(BF16)


def _mix_out(o_lat_heads, py, x, wuv_ref, wout_ref):
    o_mla = jnp.concatenate(
        [_dot(o.astype(BF16), wuv_ref[hh]) for hh, o in enumerate(o_lat_heads)], -1).astype(BF16)
    wm = o_mla.shape[-1]
    return x + _dot(o_mla, wout_ref[:wm, :]) + _dot(py, wout_ref[wm:, :])


def _attn_prompt_kernel(ql_ref, qr_ref, ct_ref, krt_ref, c_ref, x_ref, py_ref, wuv_ref, wout_ref, o_ref,
                        m_ref, l_ref, acc_ref, *, tq, heads):
    i = pl.program_id(1)
    tk = tq
    chains = [(hh, r) for hh in range(heads) for r in range(tq // ATTN_ROWS)]
    m_ref[...] = jnp.full_like(m_ref, -jnp.inf)
    l_ref[...] = jnp.zeros_like(l_ref)
    acc_ref[...] = jnp.zeros_like(acc_ref)

    def step(j, diagonal):
        off = pl.multiple_of(j * tk, tk)
        kt = ct_ref[0, :, pl.ds(off, tk)]
        rt = krt_ref[0, :, pl.ds(off, tk)]
        v = c_ref[0, pl.ds(off, tk), :]
        def scores(ci):
            hh, r = chains[ci]
            rows = pl.ds(r * ATTN_ROWS, ATTN_ROWS)
            return _dot(ql_ref[0, hh, rows, :], kt) + _dot(qr_ref[0, hh, rows, :], rt)

        pending = [scores(ci) for ci in range(ATTN_AHEAD)]
        for ci, (hh, r) in enumerate(chains):
            if ci + ATTN_AHEAD < len(chains):
                pending.append(scores(ci + ATTN_AHEAD))
            s = pending[ci]
            if diagonal:
                qpos = r * ATTN_ROWS + lax.broadcasted_iota(jnp.int32, s.shape, 0)
                kpos = lax.broadcasted_iota(jnp.int32, s.shape, 1)
                s = jnp.where(kpos <= qpos, s, -jnp.inf)
            m_prev = m_ref[ci]
            m_new = jnp.maximum(m_prev, jnp.max(s, -1, keepdims=True))
            alpha = jnp.exp(m_prev - m_new)
            p = jnp.exp(s - _lane_tile(m_new, tk // LANES))
            l_ref[ci] = alpha * l_ref[ci] + jnp.sum(p, -1, keepdims=True)
            acc_ref[ci] = (_lane_tile(alpha, acc_ref.shape[-1] // LANES) * acc_ref[ci]
                           + _dot(p.astype(BF16), v))
            m_ref[ci] = m_new

    def body(j, carry):
        step(j, False)
        return carry

    lax.fori_loop(0, i, body, 0)
    step(i, True)

    nr = tq // ATTN_ROWS
    rep = acc_ref.shape[-1] // LANES
    o_heads = [jnp.concatenate([acc_ref[hh * nr + r] / _lane_tile(l_ref[hh * nr + r], rep)
                                for r in range(nr)], 0)
               for hh in range(heads)]
    o_ref[0] = _mix_out(o_heads, py_ref[0], x_ref[0], wuv_ref, wout_ref)


def _attn_paged_kernel(pt_ref, ql_ref, qr_ref, cn_ref, krn_ref, cache_c, cache_rt, o_ref,
                       kbuf, rbuf, sem, *, n_pages):
    b = pl.program_id(0)
    nb = pl.num_programs(0)
    slot = b % 2

    def page_copies(bb, sl):
        out = []
        for p in range(n_pages):
            page = pt_ref[bb * n_pages + p]
            toks = pl.ds(p * PAGE_SIZE, PAGE_SIZE)
            out.append(pltpu.make_async_copy(cache_c.at[page], kbuf.at[sl, toks, :], sem.at[0, sl]))
            out.append(pltpu.make_async_copy(cache_rt.at[page], rbuf.at[sl, :, toks], sem.at[1, sl]))
        return out

    @pl.when(b == 0)
    def _():
        for cp in page_copies(0, 0):
            cp.start()

    @pl.when(b + 1 < nb)
    def _():
        for cp in page_copies(b + 1, 1 - slot):
            cp.start()

    for cp in page_copies(b, slot):
        cp.wait()

    ql = ql_ref[0]
    qr = qr_ref[0]
    cb = kbuf[slot].astype(BF16)
    kbt = rbuf[slot].astype(BF16)
    cn = cn_ref[0].astype(BF16).astype(F32)
    krn = krn_ref[0].astype(BF16).astype(F32)
    s = _dot_nt(ql, cb) + _dot(qr, kbt)
    s_new = (jnp.sum(ql.astype(F32) * cn, -1, keepdims=True)
             + jnp.sum(qr.astype(F32) * krn, -1, keepdims=True))
    m = jnp.maximum(jnp.max(s, -1, keepdims=True), s_new)
    p = jnp.exp(s - m)
    p_new = jnp.exp(s_new - m)
    l = jnp.sum(p, -1, keepdims=True) + p_new
    acc = _dot(p.astype(BF16), cb) + p_new.astype(BF16).astype(F32) * cn
    o_ref[0] = acc / l


def _mix_out_kernel(ol_ref, py_ref, x_ref, wuv_ref, wout_ref, o_ref, *, heads):
    o_ref[...] = _mix_out([ol_ref[hh] for hh in range(heads)], py_ref[...], x_ref[...],
                          wuv_ref, wout_ref)


def _mem_kv_kernel(mem_ref, g_ref, wk_ref, wv_ref, k_ref, v_ref, kbf_ref, vbf_ref):
    m = _rms(mem_ref[0], g_ref[...]).astype(BF16)
    k = _dot(m, wk_ref[...])
    v = _dot(m, wv_ref[...])
    k_ref[0] = k
    v_ref[0] = v
    kbf_ref[0] = k.astype(BF16)
    vbf_ref[0] = v.astype(BF16)


def _mem_attn_prompt_kernel(x_ref, g_ref, wq_ref, k_ref, v_ref, wo_ref, o_ref, *, heads, scale):
    x = x_ref[0]
    q = (_dot(_rms(x, g_ref[...]).astype(BF16), wq_ref[...]) * scale).astype(BF16)
    hd = q.shape[-1] // heads
    outs = []
    for hh in range(heads):
        sl = slice(hh * hd, (hh + 1) * hd)
        s = _dot_nt(q[:, sl], k_ref[0, :, sl])
        p = jnp.exp(s - jnp.max(s, -1, keepdims=True))
        o = _dot(p.astype(BF16), v_ref[0, :, sl]) / jnp.sum(p, -1, keepdims=True)
        outs.append(o.astype(BF16))
    o_ref[0] = x + _dot(jnp.concatenate(outs, -1), wo_ref[...])


def _mem_attn_sample_kernel(x_ref, g_ref, wq_ref, k_ref, v_ref, wo_ref, o_ref, q_ref, a_ref,
                            *, heads, scale, nb, n_mem):
    g = pl.program_id(0)
    db = x_ref.shape[0]
    rr = q_ref.shape[0] // db
    col = lambda r: (r % heads) * (rr // heads) + r // heads

    @pl.when(g == 0)
    def _():
        hq = _rms(x_ref[...], g_ref[...]).astype(BF16)
        for r in range(rr):
            j = col(r)
            q_ref[pl.ds(r, db, stride=rr), :] = _dot(hq, wq_ref[:, j * LANES:(j + 1) * LANES]) * scale

    for bb in range(nb):
        row = pl.multiple_of((g * nb + bb) * rr, rr)
        qv = q_ref[pl.ds(row, rr), :]
        k = k_ref[bb].reshape(n_mem, rr, LANES)
        part = jnp.sum(k * qv[None], -1, keepdims=True)
        s = part
        for d in range(1, rr // heads):
            s = s + pltpu.roll(part, d * heads, 1)
        p = jnp.exp(s - jnp.max(s, 0, keepdims=True))
        l = jnp.sum(p, 0)
        v = v_ref[bb].reshape(n_mem, rr, LANES)
        a_ref[pl.ds(row, rr), :] = jnp.sum(p * v, 0) / l

    @pl.when(g == pl.num_programs(0) - 1)
    def _():
        acc = x_ref[...]
        for r in range(rr):
            j = col(r)
            a = a_ref[pl.ds(r, db, stride=rr), :].astype(BF16)
            acc = acc + _dot(a, wo_ref[j * LANES:(j + 1) * LANES, :])
        o_ref[...] = acc


def _ffn_kernel(x_ref, g_ref, wg_ref, wu_ref, wd_ref, gf_ref, o_ref):
    x = x_ref[0]
    h = _rms(x, g_ref[...]).astype(BF16)
    gate = _dot(h, wg_ref[...])
    a = (gate * jax.nn.sigmoid(gate) * _dot(h, wu_ref[...])).astype(BF16)
    o_ref[0] = _rms(x + _dot(a, wd_ref[...]), gf_ref[...])


def _rope_tables(pos, rope, heads):
    half = rope // 2
    inv = jnp.power(ROPE_THETA, -(jnp.arange(half, dtype=F32) / half))
    ang = pos.astype(F32)[:, None] * inv[None, :]
    cos, sin = jnp.cos(ang), jnp.sin(ang)
    return (jnp.tile(jnp.concatenate([cos, cos], -1), (1, heads)),
            jnp.tile(jnp.concatenate([-sin, sin], -1), (1, heads)))


def _swap_halves(w):
    half = w.shape[-1] // 2
    return jnp.concatenate([w[..., half:], w[..., :half]], -1)


def _row(v):
    return v.reshape(1, -1)


def _tile(n, pref):
    return pref if n % pref == 0 else n


def _ffn(x, g, wg, wu, wd, gf, tm):
    bsz, t, d = x.shape
    spec = pl.BlockSpec((1, tm, d), lambda b, i: (b, i, 0))
    return pl.pallas_call(
        _ffn_kernel,
        grid=(bsz, t // tm),
        in_specs=[spec, _const_spec(g.shape), _const_spec(wg.shape), _const_spec(wu.shape),
                  _const_spec(wd.shape), _const_spec(gf.shape)],
        out_specs=spec,
        out_shape=jax.ShapeDtypeStruct(x.shape, F32),
        compiler_params=_params(2),
        name="ffn",
    )(x, g, wg, wu, wd, gf)


def kernel(x_prompt, x_sample, mem_prompt, cache_kv_latent, cache_k_rope, cache_pool, cache_mem_k, cache_mem_v, page_table, norm_mix, w_in, q_norm, w_uq, kv_norm, w_uk, w_uv, w_pool, pool_scale, w_out, norm_x, mem_norm, w_xq, w_mk, w_mv, w_xo, norm_ffn, w_gate, w_up, w_down, norm_final):
    depth = w_in.shape[0]
    assert depth == 1, "single-layer trunk"
    bsz, seq, d = x_prompt.shape
    db, ts, _ = x_sample.shape
    assert ts == 1, "one new token per request"
    q_lora, kv_lora = q_norm.shape[1], kv_norm.shape[1]
    heads, nope = w_uk.shape[2], w_uk.shape[3]
    rope = cache_k_rope.shape[-1]
    pool_w = pool_scale.shape[1]
    v_head = w_uv.shape[3]
    n_mem, mem_heads, mem_hd = cache_mem_k.shape[2], cache_mem_k.shape[3], cache_mem_k.shape[4]
    n_pages = page_table.shape[1]
    assert cache_kv_latent.shape[2] == PAGE_SIZE
    past = n_pages * PAGE_SIZE
    mla_scale = float((nope + rope) ** -0.5)
    mem_scale = float(mem_hd ** -0.5)
    dims = (q_lora, kv_lora, pool_w, heads, nope, rope)
    l = 0

    i1, i2, i3 = q_lora, q_lora + kv_lora, q_lora + kv_lora + rope
    wi = w_in[l]
    w_kr = wi[:, i2:i3]
    win = jnp.concatenate([wi[:, :i2], wi[:, i3:], w_kr, _swap_halves(w_kr)], -1).astype(BF16)
    wq3 = w_uq[l].reshape(q_lora, heads, nope + rope)
    wq_rope = wq3[:, :, nope:]
    wuq = jnp.concatenate([wq3[:, :, :nope].reshape(q_lora, heads * nope),
                           wq_rope.reshape(q_lora, heads * rope),
                           _swap_halves(wq_rope).reshape(q_lora, heads * rope)], -1).astype(BF16)
    wuk = jnp.transpose(w_uk[l], (1, 2, 0)).astype(BF16)
    wuv = jnp.transpose(w_uv[l], (1, 0, 2)).astype(BF16)
    wpool = w_pool[l].astype(BF16)
    wout = w_out[l].astype(BF16)
    wxq, wxo = w_xq[l].astype(BF16), w_xo[l].astype(BF16)
    wmk, wmv = w_mk[l].astype(BF16), w_mv[l].astype(BF16)
    wg, wu, wd = w_gate[l].astype(BF16), w_up[l].astype(BF16), w_down[l].astype(BF16)
    gmix, gq, gkv = _row(norm_mix[l]), _row(q_norm[l]), _row(kv_norm[l])
    gx, gmem, gffn, gfin = _row(norm_x[l]), _row(mem_norm[l]), _row(norm_ffn[l]), _row(norm_final)
    pscale = _row(pool_scale[l])
    front_consts = (gmix, win, gq, wuq, gkv, wuk)
    hr = heads * rope

    tm = _tile(seq, 512)
    cos_p, sin_p = _rope_tables(jnp.arange(seq), rope, heads)
    row_spec = lambda w: pl.BlockSpec((1, tm, w), lambda b, t: (b, t, 0))
    head_spec = lambda w: pl.BlockSpec((1, heads, tm, w), lambda b, t: (b, 0, t, 0))
    tab_spec = pl.BlockSpec((tm, hr), lambda b, t: (t, 0))
    col_spec = lambda w: pl.BlockSpec((1, w, tm), lambda b, t: (b, 0, t))
    qlat_p, qrope_p, c_p, krt_p, cbf_p, ctbf_p, krtbf_p, py_p, pst_p = pl.pallas_call(
        functools.partial(_front_prompt_kernel, scale=mla_scale, dims=dims, tm=tm),
        grid=(bsz, seq // tm),
        in_specs=[row_spec(d)] + [_const_spec(a.shape) for a in front_consts]
        + [tab_spec, tab_spec, _const_spec(wpool.shape), _const_spec(pscale.shape)],
        out_specs=[head_spec(kv_lora), head_spec(rope), row_spec(kv_lora), col_spec(rope),
                   row_spec(kv_lora), col_spec(kv_lora), col_spec(rope), row_spec(pool_w),
                   pl.BlockSpec((1, POOL_HALO, pool_w), lambda b, t: (b, 0, 0))],
        out_shape=[jax.ShapeDtypeStruct((bsz, heads, seq, kv_lora), BF16),
                   jax.ShapeDtypeStruct((bsz, heads, seq, rope), BF16),
                   jax.ShapeDtypeStruct((bsz, seq, kv_lora), F32),
                   jax.ShapeDtypeStruct((bsz, rope, seq), F32),
                   jax.ShapeDtypeStruct((bsz, seq, kv_lora), BF16),
                   jax.ShapeDtypeStruct((bsz, kv_lora, seq), BF16),
                   jax.ShapeDtypeStruct((bsz, rope, seq), BF16),
                   jax.ShapeDtypeStruct((bsz, seq, pool_w), BF16),
                   jax.ShapeDtypeStruct((bsz, POOL_HALO, pool_w), F32)],
        scratch_shapes=[pltpu.VMEM((POOL_HALO, pool_w), F32)],
        compiler_params=_params(2),
        name="front_prompt",
    )(x_prompt, *front_consts, cos_p, sin_p, wpool, pscale)

    tq = _tile(seq, 512)
    assert tq % ATTN_ROWS == 0
    n_chain = heads * (tq // ATTN_ROWS)
    q_spec = lambda w: pl.BlockSpec((1, heads, tq, w), lambda b, i: (b, 0, i, 0))
    kv_spec = lambda w: pl.BlockSpec((1, seq, w), lambda b, i: (b, 0, 0))
    kvt_spec = lambda w: pl.BlockSpec((1, w, seq), lambda b, i: (b, 0, 0))
    tq_spec = lambda w: pl.BlockSpec((1, tq, w), lambda b, i: (b, i, 0))
    x1_p = pl.pallas_call(
        functools.partial(_attn_prompt_kernel, tq=tq, heads=heads),
        grid=(bsz, seq // tq),
        in_specs=[q_spec(kv_lora), q_spec(rope), kvt_spec(kv_lora), kvt_spec(rope), kv_spec(kv_lora),
                  tq_spec(d), tq_spec(pool_w), _const_spec(wuv.shape), _const_spec(wout.shape)],
        out_specs=tq_spec(d),
        out_shape=jax.ShapeDtypeStruct((bsz, seq, d), F32),
        scratch_shapes=[pltpu.VMEM((n_chain, ATTN_ROWS, LANES), F32),
                        pltpu.VMEM((n_chain, ATTN_ROWS, LANES), F32),
                        pltpu.VMEM((n_chain, ATTN_ROWS, kv_lora), F32)],
        compiler_params=_params(2),
        name="attn_prompt",
    )(qlat_p, qrope_p, ctbf_p, krtbf_p, cbf_p, x_prompt, py_p, wuv, wout)

    mem_spec = pl.BlockSpec((1, n_mem, d), lambda b: (b, 0, 0))
    mk_p, mv_p, mkbf_p, mvbf_p = pl.pallas_call(
        _mem_kv_kernel,
        grid=(bsz,),
        in_specs=[mem_spec, _const_spec(gmem.shape), _const_spec(wmk.shape), _const_spec(wmv.shape)],
        out_specs=[mem_spec] * 4,
        out_shape=[jax.ShapeDtypeStruct((bsz, n_mem, d), F32)] * 2
        + [jax.ShapeDtypeStruct((bsz, n_mem, d), BF16)] * 2,
        compiler_params=_params(1),
        name="mem_kv",
    )(mem_prompt, gmem, wmk, wmv)

    tmem = _tile(seq, 512)
    xm_spec = pl.BlockSpec((1, tmem, d), lambda b, i: (b, i, 0))
    memkv_spec = pl.BlockSpec((1, n_mem, d), lambda b, i: (b, 0, 0))
    x2_p = pl.pallas_call(
        functools.partial(_mem_attn_prompt_kernel, heads=mem_heads, scale=mem_scale),
        grid=(bsz, seq // tmem),
        in_specs=[xm_spec, _const_spec(gx.shape), _const_spec(wxq.shape), memkv_spec, memkv_spec,
                  _const_spec(wxo.shape)],
        out_specs=xm_spec,
        out_shape=jax.ShapeDtypeStruct((bsz, seq, d), F32),
        compiler_params=_params(2),
        name="mem_attn_prompt",
    )(x1_p, gx, wxq, mkbf_p, mvbf_p, wxo)

    y_prompt = _ffn(x2_p, gffn, wg, wu, wd, gfin, _tile(seq, 256))

    xs = x_sample.reshape(db, d)
    cos_s, sin_s = _rope_tables(jnp.full((db,), past), rope, heads)
    prev_s = jnp.transpose(cache_pool[l], (1, 0, 2))
    sample_in = (xs, *front_consts, cos_s, sin_s, wpool, pscale, prev_s)
    qlat_s, qrope_s, c_s, kr_s, py_s, up_s = pl.pallas_call(
        functools.partial(_front_sample_kernel, scale=mla_scale, dims=dims, past=past),
        grid=(1,),
        in_specs=[_const_spec(a.shape) for a in sample_in],
        out_specs=[_whole_spec(s) for s in ((heads, db, kv_lora), (heads, db, rope), (db, kv_lora),
                                            (db, rope), (db, pool_w), (db, pool_w))],
        out_shape=[jax.ShapeDtypeStruct((heads, db, kv_lora), BF16),
                   jax.ShapeDtypeStruct((heads, db, rope), BF16),
                   jax.ShapeDtypeStruct((db, kv_lora), F32),
                   jax.ShapeDtypeStruct((db, rope), F32),
                   jax.ShapeDtypeStruct((db, pool_w), BF16),
                   jax.ShapeDtypeStruct((db, pool_w), F32)],
        compiler_params=_params(1),
        name="front_sample",
    )(*sample_in)

    b_spec = lambda h, w: pl.BlockSpec((1, h, w), lambda b, pt: (b, 0, 0))
    olat_s = pl.pallas_call(
        functools.partial(_attn_paged_kernel, n_pages=n_pages),
        grid_spec=pltpu.PrefetchScalarGridSpec(
            num_scalar_prefetch=1,
            grid=(db,),
            in_specs=[b_spec(heads, kv_lora), b_spec(heads, rope), b_spec(1, kv_lora), b_spec(1, rope),
                      pl.BlockSpec(memory_space=pl.ANY), pl.BlockSpec(memory_space=pl.ANY)],
            out_specs=b_spec(heads, kv_lora),
            scratch_shapes=[pltpu.VMEM((2, past, kv_lora), F32), pltpu.VMEM((2, rope, past), F32),
                            pltpu.SemaphoreType.DMA((2, 2))],
        ),
        out_shape=jax.ShapeDtypeStruct((db, heads, kv_lora), F32),
        compiler_params=_params(1),
        name="attn_paged",
    )(page_table.reshape(-1), jnp.swapaxes(qlat_s, 0, 1), jnp.swapaxes(qrope_s, 0, 1),
      c_s.reshape(db, 1, kv_lora), kr_s.reshape(db, 1, rope),
      cache_kv_latent[l], jnp.swapaxes(cache_k_rope[l], 1, 2))

    mix_in = (jnp.swapaxes(olat_s, 0, 1), py_s, xs, wuv, wout)
    x1_s = pl.pallas_call(
        functools.partial(_mix_out_kernel, heads=heads),
        grid=(1,),
        in_specs=[_const_spec(a.shape) for a in mix_in],
        out_specs=_whole_spec((db, d)),
        out_shape=jax.ShapeDtypeStruct((db, d), F32),
        compiler_params=_params(1),
        name="mix_out_sample",
    )(*mix_in)

    nb = 4
    assert db % nb == 0 and mem_hd % LANES == 0
    slab = mem_heads * (mem_hd // LANES)

    def slab_view(a):
        a = a.reshape(db, n_mem, mem_heads, mem_hd // LANES, LANES)
        return jnp.transpose(a, (0, 1, 3, 2, 4)).reshape(db, n_mem * slab, LANES)

    cache_spec = pl.BlockSpec((nb, n_mem * slab, LANES), lambda g: (g, 0, 0))
    x2_s = pl.pallas_call(
        functools.partial(_mem_attn_sample_kernel, heads=mem_heads, scale=mem_scale, nb=nb, n_mem=n_mem),
        grid=(db // nb,),
        in_specs=[_const_spec((db, d)), _const_spec(gx.shape), _const_spec(wxq.shape), cache_spec,
                  cache_spec, _const_spec(wxo.shape)],
        out_specs=_whole_spec((db, d)),
        out_shape=jax.ShapeDtypeStruct((db, d), F32),
        scratch_shapes=[pltpu.VMEM((db * slab, LANES), F32), pltpu.VMEM((db * slab, LANES), F32)],
        compiler_params=_params(1),
        name="mem_attn_sample",
    )(x1_s, gx, wxq, slab_view(cache_mem_k[l]), slab_view(cache_mem_v[l]), wxo)

    y_sample = _ffn(x2_s.reshape(1, db, d), gffn, wg, wu, wd, gfin, db).reshape(db, ts, d)

    mem_shape = (depth, bsz, n_mem, mem_heads, mem_hd)
    return (y_prompt, y_sample,
            c_p[None], jnp.swapaxes(krt_p, 1, 2)[None], pst_p[None, :, POOL_HALO - POOL_STATE:],
            mk_p.reshape(mem_shape), mv_p.reshape(mem_shape),
            c_s.reshape(depth, db, ts, kv_lora), kr_s.reshape(depth, db, ts, rope),
            jnp.concatenate([cache_pool[l][:, 1:], up_s[:, None]], 1)[None])
```

```python
import functools

import jax
import jax.numpy as jnp
from jax import lax
from jax.experimental import pallas as pl
from jax.experimental.pallas import tpu as pltpu

F32 = jnp.float32
BF16 = jnp.bfloat16

EPS = 1e-6
ROPE_THETA = 10000.0
PAGE_SIZE = 128
POOL_WINDOWS = (2, 4, 8, 16)
POOL_STATE = max(POOL_WINDOWS) - 1
POOL_HALO = 16
LANES = 128
VMEM_LIMIT = 56 * 1024 * 1024
ATTN_ROWS = 128
ATTN_AHEAD = 6


def _rms(x, g):
    return x * lax.rsqrt(jnp.mean(x * x, -1, keepdims=True) + EPS) * g


def _dot(a, b):
    return jnp.dot(a, b, preferred_element_type=F32)


def _dot_nt(a, b):
    return lax.dot_general(a, b, (((1,), (1,)), ((), ())), preferred_element_type=F32)


def _lane_tile(x, n):
    return jnp.concatenate([x] * n, -1)


def _const_spec(shape):
    nd = len(shape)
    return pl.BlockSpec(shape, lambda *_: (0,) * nd, pipeline_mode=pl.Buffered(1))


def _whole_spec(shape):
    nd = len(shape)
    return pl.BlockSpec(shape, lambda *_: (0,) * nd)


def _params(n_axes):
    return pltpu.CompilerParams(dimension_semantics=("arbitrary",) * n_axes,
                                vmem_limit_bytes=VMEM_LIMIT)


def _mla_inputs(x, gmix, win, gq, wuq, gkv, wuk, cos, sin, scale, dims):
    q_lora, kv_lora, pool_w, heads, nope, rope = dims
    h = _rms(x, gmix).astype(BF16)
    u = _dot(h, win)
    i1, i2, i3 = q_lora, q_lora + kv_lora, q_lora + kv_lora + pool_w
    cq, ckv, up, krs = u[:, :i1], u[:, i1:i2], u[:, i2:i3], u[:, i3:]
    q = _dot(_rms(cq, gq).astype(BF16), wuq)
    n0 = heads * nope
    n1 = n0 + heads * rope
    q_rope = (q[:, n0:n1] * cos + q[:, n1:] * sin) * scale
    q_lat = [(_dot(q[:, hh * nope:(hh + 1) * nope].astype(BF16), wuk[hh]) * scale).astype(BF16)
             for hh in range(heads)]
    q_rope = [q_rope[:, hh * rope:(hh + 1) * rope].astype(BF16) for hh in range(heads)]
    c = _rms(ckv, gkv)
    k_rope = krs * cos[:, :2 * rope] + pltpu.roll(krs, rope, 1) * sin[:, :2 * rope]
    return q_lat, q_rope, c, k_rope, up


def _pool_project(sums, up, cnt, wpool, pscale):
    cg = wpool.shape[-1]
    ys = []
    for g in range(len(POOL_WINDOWS)):
        m = sums[g] / cnt[g] - up[:, g * cg:(g + 1) * cg]
        ys.append(_dot(m.astype(BF16), wpool[g]))
    return jnp.concatenate(ys, -1) * pscale


def _front_prompt_kernel(x_ref, gmix_ref, win_ref, gq_ref, wuq_ref, gkv_ref, wuk_ref, cos_ref, sin_ref,
                         wpool_ref, pscale_ref,
                         qlat_ref, qrope_ref, c_ref, krt_ref, cbf_ref, ctbf_ref, krtbf_ref, py_ref, pst_ref,
                         prev_ref, *, scale, dims, tm):
    t = pl.program_id(1)
    heads, rope = dims[3], dims[5]
    q_lat, q_rope, c, k_rope, up = _mla_inputs(
        x_ref[0], gmix_ref[...], win_ref[...], gq_ref[...], wuq_ref[...], gkv_ref[...], wuk_ref,
        cos_ref[...], sin_ref[...], scale, dims)
    for hh in range(heads):
        qlat_ref[0, hh] = q_lat[hh]
        qrope_ref[0, hh] = q_rope[hh]
    c_ref[0] = c
    cbf_ref[0] = c.astype(BF16)
    ctbf_ref[0] = c.T.astype(BF16)
    krt = k_rope.T[:rope]
    krt_ref[0] = krt
    krtbf_ref[0] = krt.astype(BF16)

    @pl.when(t == 0)
    def _():
        prev_ref[...] = jnp.zeros_like(prev_ref)

    cg = wpool_ref.shape[-1]
    e = jnp.concatenate([prev_ref[...], up], 0)
    tail = up[tm - POOL_HALO:, :]
    prev_ref[...] = tail
    pst_ref[0] = tail
    sums = []
    s = e
    for g, w in enumerate(POOL_WINDOWS):
        s = s[:, (cg if g else 0):]
        s = s + pltpu.roll(s, w // 2, 0)
        sums.append(s[POOL_HALO:, :cg])
    pos1 = t * tm + lax.broadcasted_iota(jnp.int32, (tm, cg), 0) + 1
    cnt = [jnp.minimum(w, pos1).astype(F32) for w in POOL_WINDOWS]
    py_ref[0] = _pool_project(sums, up, cnt, wpool_ref, pscale_ref[...]).astype(BF16)


def _front_sample_kernel(x_ref, gmix_ref, win_ref, gq_ref, wuq_ref, gkv_ref, wuk_ref, cos_ref, sin_ref,
                         wpool_ref, pscale_ref, prev_ref,
                         qlat_ref, qrope_ref, c_ref, kr_ref, py_ref, up_ref, *, scale, dims, past):
    heads, rope = dims[3], dims[5]
    q_lat, q_rope, c, k_rope, up = _mla_inputs(
        x_ref[...], gmix_ref[...], win_ref[...], gq_ref[...], wuq_ref[...], gkv_ref[...], wuk_ref,
        cos_ref[...], sin_ref[...], scale, dims)
    for hh in range(heads):
        qlat_ref[hh] = q_lat[hh]
        qrope_ref[hh] = q_rope[hh]
    c_ref[...] = c
    kr_ref[...] = k_rope[:, :rope]
    up_ref[...] = up

    cg = wpool_ref.shape[-1]
    sums, cnt = [], []
    for g, w in enumerate(POOL_WINDOWS):
        s = up[:, g * cg:(g + 1) * cg]
        for j in range(1, w):
            s = s + prev_ref[POOL_STATE - j][:, g * cg:(g + 1) * cg]
        sums.append(s)
        cnt.append(float(min(w, past + 1)))
    py_ref[...] = _pool_project(sums, up, cnt, wpool_ref, pscale_ref[...]).astype(BF16)


def _mix_out(o_lat_heads, py, x, wuv_ref, wout_ref):
    o_mla = jnp.concatenate(
        [_dot(o.astype(BF16), wuv_ref[hh]) for hh, o in enumerate(o_lat_heads)], -1).astype(BF16)
    wm = o_mla.shape[-1]
    return x + _dot(o_mla, wout_ref[:wm, :]) + _dot(py, wout_ref[wm:, :])


def _attn_prompt_kernel(ql_ref, qr_ref, ct_ref, krt_ref, c_ref, x_ref, py_ref, wuv_ref, wout_ref, o_ref,
                        m_ref, l_ref, acc_ref, *, tq, heads):
    i = pl.program_id(1)
    tk = tq
    chains = [(hh, r) for hh in range(heads) for r in range(tq // ATTN_ROWS)]
    m_ref[...] = jnp.full_like(m_ref, -jnp.inf)
    l_ref[...] = jnp.zeros_like(l_ref)
    acc_ref[...] = jnp.zeros_like(acc_ref)

    def step(j, diagonal):
        off = pl.multiple_of(j * tk, tk)
        kt = ct_ref[0, :, pl.ds(off, tk)]
        rt = krt_ref[0, :, pl.ds(off, tk)]
        v = c_ref[0, pl.ds(off, tk), :]
        def scores(ci):
            hh, r = chains[ci]
            rows = pl.ds(r * ATTN_ROWS, ATTN_ROWS)
            return _dot(ql_ref[0, hh, rows, :], kt) + _dot(qr_ref[0, hh, rows, :], rt)

        pending = [scores(ci) for ci in range(ATTN_AHEAD)]
        for ci, (hh, r) in enumerate(chains):
            if ci + ATTN_AHEAD < len(chains):
                pending.append(scores(ci + ATTN_AHEAD))
            s = pending[ci]
            if diagonal:
                qpos = r * ATTN_ROWS + lax.broadcasted_iota(jnp.int32, s.shape, 0)
                kpos = lax.broadcasted_iota(jnp.int32, s.shape, 1)
                s = jnp.where(kpos <= qpos, s, -jnp.inf)
            m_prev = m_ref[ci]
            m_new = jnp.maximum(m_prev, jnp.max(s, -1, keepdims=True))
            alpha = jnp.exp(m_prev - m_new)
            p = jnp.exp(s - _lane_tile(m_new, tk // LANES))
            l_ref[ci] = alpha * l_ref[ci] + jnp.sum(p, -1, keepdims=True)
            acc_ref[ci] = (_lane_tile(alpha, acc_ref.shape[-1] // LANES) * acc_ref[ci]
                           + _dot(p.astype(BF16), v))
            m_ref[ci] = m_new

    def body(j, carry):
        step(j, False)
        return carry

    lax.fori_loop(0, i, body, 0)
    step(i, True)

    nr = tq // ATTN_ROWS
    rep = acc_ref.shape[-1] // LANES
    o_heads = [jnp.concatenate([acc_ref[hh * nr + r] / _lane_tile(l_ref[hh * nr + r], rep)
                                for r in range(nr)], 0)
               for hh in range(heads)]
    o_ref[0] = _mix_out(o_heads, py_ref[0], x_ref[0], wuv_ref, wout_ref)


def _attn_paged_kernel(pt_ref, ql_ref, qr_ref, cn_ref, krn_ref, cache_c, cache_rt, o_ref,
                       kbuf, rbuf, sem, *, n_pages):
    b = pl.program_id(0)
    nb = pl.num_programs(0)
    slot = b % 2

    def page_copies(bb, sl):
        out = []
        for p in range(n_pages):
            page = pt_ref[bb * n_pages + p]
            toks = pl.ds(p * PAGE_SIZE, PAGE_SIZE)
            out.append(pltpu.make_async_copy(cache_c.at[page], kbuf.at[sl, toks, :], sem.at[0, sl]))
            out.append(pltpu.make_async_copy(cache_rt.at[page], rbuf.at[sl, :, toks], sem.at[1, sl]))
        return out

    @pl.when(b == 0)
    def _():
        for cp in page_copies(0, 0):
            cp.start()

    @pl.when(b + 1 < nb)
    def _():
        for cp in page_copies(b + 1, 1 - slot):
            cp.start()

    for cp in page_copies(b, slot):
        cp.wait()

    ql = ql_ref[0]
    qr = qr_ref[0]
    cb = kbuf[slot].astype(BF16)
    kbt = rbuf[slot].astype(BF16)
    cn = cn_ref[0].astype(BF16).astype(F32)
    krn = krn_ref[0].astype(BF16).astype(F32)
    s = _dot_nt(ql, cb) + _dot(qr, kbt)
    s_new = (jnp.sum(ql.astype(F32) * cn, -1, keepdims=True)
             + jnp.sum(qr.astype(F32) * krn, -1, keepdims=True))
    m = jnp.maximum(jnp.max(s, -1, keepdims=True), s_new)
    p = jnp.exp(s - m)
    p_new = jnp.exp(s_new - m)
    l = jnp.sum(p, -1, keepdims=True) + p_new
    acc = _dot(p.astype(BF16), cb) + p_new.astype(BF16).astype(F32) * cn
    o_ref[0] = acc / l


def _mix_out_kernel(ol_ref, py_ref, x_ref, wuv_ref, wout_ref, o_ref, *, heads):
    o_ref[...] = _mix_out([ol_ref[hh] for hh in range(heads)], py_ref[...], x_ref[...],
                          wuv_ref, wout_ref)


def _mem_kv_kernel(mem_ref, g_ref, wk_ref, wv_ref, k_ref, v_ref, kbf_ref, vbf_ref):
    m = _rms(mem_ref[0], g_ref[...]).astype(BF16)
    k = _dot(m, wk_ref[...])
    v = _dot(m, wv_ref[...])
    k_ref[0] = k
    v_ref[0] = v
    kbf_ref[0] = k.astype(BF16)
    vbf_ref[0] = v.astype(BF16)


def _mem_attn_prompt_kernel(x_ref, g_ref, wq_ref, k_ref, v_ref, wo_ref, o_ref, *, heads, scale):
    x = x_ref[0]
    q = (_dot(_rms(x, g_ref[...]).astype(BF16), wq_ref[...]) * scale).astype(BF16)
    hd = q.shape[-1] // heads
    outs = []
    for hh in range(heads):
        sl = slice(hh * hd, (hh + 1) * hd)
        s = _dot_nt(q[:, sl], k_ref[0, :, sl])
        p = jnp.exp(s - jnp.max(s, -1, keepdims=True))
        o = _dot(p.astype(BF16), v_ref[0, :, sl]) / jnp.sum(p, -1, keepdims=True)
        outs.append(o.astype(BF16))
    o_ref[0] = x + _dot(jnp.concatenate(outs, -1), wo_ref[...])


def _mem_attn_sample_kernel(x_ref, g_ref, wq_ref, k_ref, v_ref, wo_ref, o_ref, q_ref, a_ref,
                            *, heads, scale, nb, n_mem):
    g = pl.program_id(0)
    db = x_ref.shape[0]
    rr = q_ref.shape[0] // db
    col = lambda r: (r % heads) * (rr // heads) + r // heads

    @pl.when(g == 0)
    def _():
        hq = _rms(x_ref[...], g_ref[...]).astype(BF16)
        for r in range(rr):
            j = col(r)
            q_ref[pl.ds(r, db, stride=rr), :] = _dot(hq, wq_ref[:, j * LANES:(j + 1) * LANES]) * scale

    for bb in range(nb):
        row = pl.multiple_of((g * nb + bb) * rr, rr)
        qv = q_ref[pl.ds(row, rr), :]
        k = k_ref[bb].reshape(n_mem, rr, LANES)
        part = jnp.sum(k * qv[None], -1, keepdims=True)
        s = part
        for d in range(1, rr // heads):
            s = s + pltpu.roll(part, d * heads, 1)
        p = jnp.exp(s - jnp.max(s, 0, keepdims=True))
        l = jnp.sum(p, 0)
        v = v_ref[bb].reshape(n_mem, rr, LANES)
        a_ref[pl.ds(row, rr), :] = jnp.sum(p * v, 0) / l

    @pl.when(g == pl.num_programs(0) - 1)
    def _():
        acc = x_ref[...]
        for r in range(rr):
            j = col(r)
            a = a_ref[pl.ds(r, db, stride=rr), :].astype(BF16)
            acc = acc + _dot(a, wo_ref[j * LANES:(j + 1) * LANES, :])
        o_ref[...] = acc


def _ffn_paged_kernel(pt_ref, x_ref, g_ref, wg_ref, wu_ref, wd_ref, gf_ref,
                      ql_ref, qr_ref, cn_ref, krn_ref, cache_c, cache_rt,
                      y_ref, o_ref, kbuf, rbuf, sem, m_ref, l_ref, acc_ref,
                      *, n_pages, bps, cpb, ff_parts):
    s_id = pl.program_id(0)
    n_steps = pl.num_programs(0)
    cps = bps * cpb
    cpages = n_pages // cpb

    def chunk_copies(step, k, sl):
        req, part = divmod(k, cpb)
        base = (step * bps + req) * n_pages + part * cpages
        out = []
        for p in range(cpages):
            page = pt_ref[base + p]
            toks = pl.ds(p * PAGE_SIZE, PAGE_SIZE)
            out.append(pltpu.make_async_copy(cache_c.at[page], kbuf.at[sl, toks, :], sem.at[0, sl]))
            out.append(pltpu.make_async_copy(cache_rt.at[page], rbuf.at[sl, :, toks], sem.at[1, sl]))
        return out

    @pl.when(s_id == 0)
    def _():
        for k in range(2):
            for cp in chunk_copies(0, k, k):
                cp.start()

    x = x_ref[...]
    h = _rms(x, g_ref[...]).astype(BF16)
    y = x
    for k in range(cps):
        sl = k % 2
        req, part = divmod(k, cpb)
        for cp in chunk_copies(s_id, k, sl):
            cp.wait()

        c0, c1 = ff_parts[k], ff_parts[k + 1]
        gate = _dot(h, wg_ref[:, c0:c1])
        a = (gate * jax.nn.sigmoid(gate) * _dot(h, wu_ref[:, c0:c1])).astype(BF16)
        y = y + _dot(a, wd_ref[c0:c1, :])

        ql = ql_ref[req]
        qr = qr_ref[req]
        cb = kbuf[sl].astype(BF16)
        kbt = rbuf[sl].astype(BF16)
        s = _dot_nt(ql, cb) + _dot(qr, kbt)
        if part == 0:
            cn = cn_ref[req].astype(BF16).astype(F32)
            krn = krn_ref[req].astype(BF16).astype(F32)
            m_prev = (jnp.sum(ql.astype(F32) * cn, -1, keepdims=True)
                      + jnp.sum(qr.astype(F32) * krn, -1, keepdims=True))
            l_prev = jnp.ones_like(m_prev)
            acc_prev = jnp.broadcast_to(cn, (ql.shape[0], cn.shape[-1]))
        else:
            m_prev, l_prev, acc_prev = m_ref[...], l_ref[...], acc_ref[...]
        m_new = jnp.maximum(m_prev, jnp.max(s, -1, keepdims=True))
        alpha = jnp.exp(m_prev - m_new)
        p = jnp.exp(s - m_new)
        l_new = alpha * l_prev + jnp.sum(p, -1, keepdims=True)
        acc_new = alpha * acc_prev + _dot(p.astype(BF16), cb)
        if part == cpb - 1:
            o_ref[req] = acc_new / l_new
        else:
            m_ref[...], l_ref[...], acc_ref[...] = m_new, l_new, acc_new

        nk = k + 2
        if nk < cps:
            for cp in chunk_copies(s_id, nk, sl):
                cp.start()
        else:
            @pl.when(s_id + 1 < n_steps)
            def _():
                for cp in chunk_copies(s_id + 1, nk - cps, sl):
                    cp.start()

    y_ref[...] = _rms(y, gf_ref[...])


def _ffn_kernel(x_ref, g_ref, wg_ref, wu_ref, wd_ref, gf_ref, o_ref):
    x = x_ref[0]
    h = _rms(x, g_ref[...]).astype(BF16)
    gate = _dot(h, wg_ref[...])
    a = (gate * jax.nn.sigmoid(gate) * _dot(h, wu_ref[...])).astype(BF16)
    o_ref[0] = _rms(x + _dot(a, wd_ref[...]), gf_ref[...])


def _rope_tables(pos, rope, heads):
    half = rope // 2
    inv = jnp.power(ROPE_THETA, -(jnp.arange(half, dtype=F32) / half))
    ang = pos.astype(F32)[:, None] * inv[None, :]
    cos, sin = jnp.cos(ang), jnp.sin(ang)
    return (jnp.tile(jnp.concatenate([cos, cos], -1), (1, heads)),
            jnp.tile(jnp.concatenate([-sin, sin], -1), (1, heads)))


def _swap_halves(w):
    half = w.shape[-1] // 2
    return jnp.concatenate([w[..., half:], w[..., :half]], -1)


def _row(v):
    return v.reshape(1, -1)


def _tile(n, pref):
    return pref if n % pref == 0 else n


def _ffn(x, g, wg, wu, wd, gf, tm):
    bsz, t, d = x.shape
    spec = pl.BlockSpec((1, tm, d), lambda b, i: (b, i, 0))
    return pl.pallas_call(
        _ffn_kernel,
        grid=(bsz, t // tm),
        in_specs=[spec, _const_spec(g.shape), _const_spec(wg.shape), _const_spec(wu.shape),
                  _const_spec(wd.shape), _const_spec(gf.shape)],
        out_specs=spec,
        out_shape=jax.ShapeDtypeStruct(x.shape, F32),
        compiler_params=_params(2),
        name="ffn",
    )(x, g, wg, wu, wd, gf)


def _split_even(total, parts, unit):
    assert total % unit == 0 and total // unit >= parts
    n = total // unit
    sizes = [n // parts + (1 if i < n % parts else 0) for i in range(parts)]
    bounds = [0]
    for sz in sizes:
        bounds.append(bounds[-1] + sz * unit)
    return tuple(bounds)


def _ffn_with_paged_attention(x, g, wg, wu, wd, gf, page_table, qlat, qrope, c_new, kr_new, cache_c, cache_rt,
                              tm, cpb):
    rows, d = x.shape
    db, heads, kv = qlat.shape
    rope = qrope.shape[-1]
    n_pages = page_table.shape[1]
    n_steps = rows // tm
    assert rows % tm == 0 and db % n_steps == 0 and n_pages % cpb == 0
    bps = db // n_steps
    assert (bps * cpb) % 2 == 0, "chunk k must map to ring slot k % 2 in every step"
    chunk = n_pages // cpb * PAGE_SIZE
    ff_parts = _split_even(wg.shape[1], bps * cpb, 2 * LANES)
    x_spec = pl.BlockSpec((tm, d), lambda s, pt: (s, 0))
    req_spec = lambda a, w: pl.BlockSpec((bps, a, w), lambda s, pt: (s, 0, 0))
    return pl.pallas_call(
        functools.partial(_ffn_paged_kernel, n_pages=n_pages, bps=bps, cpb=cpb, ff_parts=ff_parts),
        grid_spec=pltpu.PrefetchScalarGridSpec(
            num_scalar_prefetch=1,
            grid=(n_steps,),
            in_specs=[x_spec, _const_spec(g.shape), _const_spec(wg.shape), _const_spec(wu.shape),
                      _const_spec(wd.shape), _const_spec(gf.shape),
                      req_spec(heads, kv), req_spec(heads, rope), req_spec(1, kv), req_spec(1, rope),
                      pl.BlockSpec(memory_space=pl.ANY), pl.BlockSpec(memory_space=pl.ANY)],
            out_specs=[x_spec, req_spec(heads, kv)],
            scratch_shapes=[pltpu.VMEM((2, chunk, kv), F32), pltpu.VMEM((2, rope, chunk), F32),
                            pltpu.SemaphoreType.DMA((2, 2)),
                            pltpu.VMEM((heads, 1), F32), pltpu.VMEM((heads, 1), F32),
                            pltpu.VMEM((heads, kv), F32)],
        ),
        out_shape=[jax.ShapeDtypeStruct((rows, d), F32), jax.ShapeDtypeStruct((db, heads, kv), F32)],
        compiler_params=_params(1),
        name="ffn_paged",
    )(page_table.reshape(-1), x, g, wg, wu, wd, gf, qlat, qrope, c_new, kr_new, cache_c, cache_rt)


def kernel(x_prompt, x_sample, mem_prompt, cache_kv_latent, cache_k_rope, cache_pool, cache_mem_k, cache_mem_v, page_table, norm_mix, w_in, q_norm, w_uq, kv_norm, w_uk, w_uv, w_pool, pool_scale, w_out, norm_x, mem_norm, w_xq, w_mk, w_mv, w_xo, norm_ffn, w_gate, w_up, w_down, norm_final):
    depth = w_in.shape[0]
    assert depth == 1, "single-layer trunk"
    bsz, seq, d = x_prompt.shape
    db, ts, _ = x_sample.shape
    assert ts == 1, "one new token per request"
    q_lora, kv_lora = q_norm.shape[1], kv_norm.shape[1]
    heads, nope = w_uk.shape[2], w_uk.shape[3]
    rope = cache_k_rope.shape[-1]
    pool_w = pool_scale.shape[1]
    v_head = w_uv.shape[3]
    n_mem, mem_heads, mem_hd = cache_mem_k.shape[2], cache_mem_k.shape[3], cache_mem_k.shape[4]
    n_pages = page_table.shape[1]
    assert cache_kv_latent.shape[2] == PAGE_SIZE
    past = n_pages * PAGE_SIZE
    mla_scale = float((nope + rope) ** -0.5)
    mem_scale = float(mem_hd ** -0.5)
    dims = (q_lora, kv_lora, pool_w, heads, nope, rope)
    l = 0

    i1, i2, i3 = q_lora, q_lora + kv_lora, q_lora + kv_lora + rope
    wi = w_in[l]
    w_kr = wi[:, i2:i3]
    win = jnp.concatenate([wi[:, :i2], wi[:, i3:], w_kr, _swap_halves(w_kr)], -1).astype(BF16)
    wq3 = w_uq[l].reshape(q_lora, heads, nope + rope)
    wq_rope = wq3[:, :, nope:]
    wuq = jnp.concatenate([wq3[:, :, :nope].reshape(q_lora, heads * nope),
                           wq_rope.reshape(q_lora, heads * rope),
                           _swap_halves(wq_rope).reshape(q_lora, heads * rope)], -1).astype(BF16)
    wuk = jnp.transpose(w_uk[l], (1, 2, 0)).astype(BF16)
    wuv = jnp.transpose(w_uv[l], (1, 0, 2)).astype(BF16)
    wpool = w_pool[l].astype(BF16)
    wout = w_out[l].astype(BF16)
    wxq, wxo = w_xq[l].astype(BF16), w_xo[l].astype(BF16)
    wmk, wmv = w_mk[l].astype(BF16), w_mv[l].astype(BF16)
    wg, wu, wd = w_gate[l].astype(BF16), w_up[l].astype(BF16), w_down[l].astype(BF16)
    gmix, gq, gkv = _row(norm_mix[l]), _row(q_norm[l]), _row(kv_norm[l])
    gx, gmem, gffn, gfin = _row(norm_x[l]), _row(mem_norm[l]), _row(norm_ffn[l]), _row(norm_final)
    pscale = _row(pool_scale[l])
    front_consts = (gmix, win, gq, wuq, gkv, wuk)
    hr = heads * rope

    tm = _tile(seq, 512)
    cos_p, sin_p = _rope_tables(jnp.arange(seq), rope, heads)
    row_spec = lambda w: pl.BlockSpec((1, tm, w), lambda b, t: (b, t, 0))
    head_spec = lambda w: pl.BlockSpec((1, heads, tm, w), lambda b, t: (b, 0, t, 0))
    tab_spec = pl.BlockSpec((tm, hr), lambda b, t: (t, 0))
    col_spec = lambda w: pl.BlockSpec((1, w, tm), lambda b, t: (b, 0, t))
    qlat_p, qrope_p, c_p, krt_p, cbf_p, ctbf_p, krtbf_p, py_p, pst_p = pl.pallas_call(
        functools.partial(_front_prompt_kernel, scale=mla_scale, dims=dims, tm=tm),
        grid=(bsz, seq // tm),
        in_specs=[row_spec(d)] + [_const_spec(a.shape) for a in front_consts]
        + [tab_spec, tab_spec, _const_spec(wpool.shape), _const_spec(pscale.shape)],
        out_specs=[head_spec(kv_lora), head_spec(rope), row_spec(kv_lora), col_spec(rope),
                   row_spec(kv_lora), col_spec(kv_lora), col_spec(rope), row_spec(pool_w),
                   pl.BlockSpec((1, POOL_HALO, pool_w), lambda b, t: (b, 0, 0))],
        out_shape=[jax.ShapeDtypeStruct((bsz, heads, seq, kv_lora), BF16),
                   jax.ShapeDtypeStruct((bsz, heads, seq, rope), BF16),
                   jax.ShapeDtypeStruct((bsz, seq, kv_lora), F32),
                   jax.ShapeDtypeStruct((bsz, rope, seq), F32),
                   jax.ShapeDtypeStruct((bsz, seq, kv_lora), BF16),
                   jax.ShapeDtypeStruct((bsz, kv_lora, seq), BF16),
                   jax.ShapeDtypeStruct((bsz, rope, seq), BF16),
                   jax.ShapeDtypeStruct((bsz, seq, pool_w), BF16),
                   jax.ShapeDtypeStruct((bsz, POOL_HALO, pool_w), F32)],
        scratch_shapes=[pltpu.VMEM((POOL_HALO, pool_w), F32)],
        compiler_params=_params(2),
        name="front_prompt",
    )(x_prompt, *front_consts, cos_p, sin_p, wpool, pscale)

    tq = _tile(seq, 512)
    assert tq % ATTN_ROWS == 0
    n_chain = heads * (tq // ATTN_ROWS)
    q_spec = lambda w: pl.BlockSpec((1, heads, tq, w), lambda b, i: (b, 0, i, 0))
    kv_spec = lambda w: pl.BlockSpec((1, seq, w), lambda b, i: (b, 0, 0))
    kvt_spec = lambda w: pl.BlockSpec((1, w, seq), lambda b, i: (b, 0, 0))
    tq_spec = lambda w: pl.BlockSpec((1, tq, w), lambda b, i: (b, i, 0))
    x1_p = pl.pallas_call(
        functools.partial(_attn_prompt_kernel, tq=tq, heads=heads),
        grid=(bsz, seq // tq),
        in_specs=[q_spec(kv_lora), q_spec(rope), kvt_spec(kv_lora), kvt_spec(rope), kv_spec(kv_lora),
                  tq_spec(d), tq_spec(pool_w), _const_spec(wuv.shape), _const_spec(wout.shape)],
        out_specs=tq_spec(d),
        out_shape=jax.ShapeDtypeStruct((bsz, seq, d), F32),
        scratch_shapes=[pltpu.VMEM((n_chain, ATTN_ROWS, LANES), F32),
                        pltpu.VMEM((n_chain, ATTN_ROWS, LANES), F32),
                        pltpu.VMEM((n_chain, ATTN_ROWS, kv_lora), F32)],
        compiler_params=_params(2),
        name="attn_prompt",
    )(qlat_p, qrope_p, ctbf_p, krtbf_p, cbf_p, x_prompt, py_p, wuv, wout)

    mem_spec = pl.BlockSpec((1, n_mem, d), lambda b: (b, 0, 0))
    mk_p, mv_p, mkbf_p, mvbf_p = pl.pallas_call(
        _mem_kv_kernel,
        grid=(bsz,),
        in_specs=[mem_spec, _const_spec(gmem.shape), _const_spec(wmk.shape), _const_spec(wmv.shape)],
        out_specs=[mem_spec] * 4,
        out_shape=[jax.ShapeDtypeStruct((bsz, n_mem, d), F32)] * 2
        + [jax.ShapeDtypeStruct((bsz, n_mem, d), BF16)] * 2,
        compiler_params=_params(1),
        name="mem_kv",
    )(mem_prompt, gmem, wmk, wmv)

    tmem = _tile(seq, 512)
    xm_spec = pl.BlockSpec((1, tmem, d), lambda b, i: (b, i, 0))
    memkv_spec = pl.BlockSpec((1, n_mem, d), lambda b, i: (b, 0, 0))
    x2_p = pl.pallas_call(
        functools.partial(_mem_attn_prompt_kernel, heads=mem_heads, scale=mem_scale),
        grid=(bsz, seq // tmem),
        in_specs=[xm_spec, _const_spec(gx.shape), _const_spec(wxq.shape), memkv_spec, memkv_spec,
                  _const_spec(wxo.shape)],
        out_specs=xm_spec,
        out_shape=jax.ShapeDtypeStruct((bsz, seq, d), F32),
        compiler_params=_params(2),
        name="mem_attn_prompt",
    )(x1_p, gx, wxq, mkbf_p, mvbf_p, wxo)

    xs = x_sample.reshape(db, d)
    cos_s, sin_s = _rope_tables(jnp.full((db,), past), rope, heads)
    prev_s = jnp.transpose(cache_pool[l], (1, 0, 2))
    sample_in = (xs, *front_consts, cos_s, sin_s, wpool, pscale, prev_s)
    qlat_s, qrope_s, c_s, kr_s, py_s, up_s = pl.pallas_call(
        functools.partial(_front_sample_kernel, scale=mla_scale, dims=dims, past=past),
        grid=(1,),
        in_specs=[_const_spec(a.shape) for a in sample_in],
        out_specs=[_whole_spec(s) for s in ((heads, db, kv_lora), (heads, db, rope), (db, kv_lora),
                                            (db, rope), (db, pool_w), (db, pool_w))],
        out_shape=[jax.ShapeDtypeStruct((heads, db, kv_lora), BF16),
                   jax.ShapeDtypeStruct((heads, db, rope), BF16),
                   jax.ShapeDtypeStruct((db, kv_lora), F32),
                   jax.ShapeDtypeStruct((db, rope), F32),
                   jax.ShapeDtypeStruct((db, pool_w), BF16),
                   jax.ShapeDtypeStruct((db, pool_w), F32)],
        compiler_params=_params(1),
        name="front_sample",
    )(*sample_in)

    y_prompt, olat_s = _ffn_with_paged_attention(
        x2_p.reshape(bsz * seq, d), gffn, wg, wu, wd, gfin, page_table,
        jnp.swapaxes(qlat_s, 0, 1), jnp.swapaxes(qrope_s, 0, 1),
        c_s.reshape(db, 1, kv_lora), kr_s.reshape(db, 1, rope),
        cache_kv_latent[l], jnp.swapaxes(cache_k_rope[l], 1, 2),
        tm=_tile(bsz * seq, 256), cpb=2)
    y_prompt = y_prompt.reshape(bsz, seq, d)

    mix_in = (jnp.swapaxes(olat_s, 0, 1), py_s, xs, wuv, wout)
    x1_s = pl.pallas_call(
        functools.partial(_mix_out_kernel, heads=heads),
        grid=(1,),
        in_specs=[_const_spec(a.shape) for a in mix_in],
        out_specs=_whole_spec((db, d)),
        out_shape=jax.ShapeDtypeStruct((db, d), F32),
        compiler_params=_params(1),
        name="mix_out_sample",
    )(*mix_in)

    nb = 4
    assert db % nb == 0 and mem_hd % LANES == 0
    slab = mem_heads * (mem_hd // LANES)

    def slab_view(a):
        a = a.reshape(db, n_mem, mem_heads, mem_hd // LANES, LANES)
        return jnp.transpose(a, (0, 1, 3, 2, 4)).reshape(db, n_mem * slab, LANES)

    cache_spec = pl.BlockSpec((nb, n_mem * slab, LANES), lambda g: (g, 0, 0))
    x2_s = pl.pallas_call(
        functools.partial(_mem_attn_sample_kernel, heads=mem_heads, scale=mem_scale, nb=nb, n_mem=n_mem),
        grid=(db // nb,),
        in_specs=[_const_spec((db, d)), _const_spec(gx.shape), _const_spec(wxq.shape), cache_spec,
                  cache_spec, _const_spec(wxo.shape)],
        out_specs=_whole_spec((db, d)),
        out_shape=jax.ShapeDtypeStruct((db, d), F32),
        scratch_shapes=[pltpu.VMEM((db * slab, LANES), F32), pltpu.VMEM((db * slab, LANES), F32)],
        compiler_params=_params(1),
        name="mem_attn_sample",
    )(x1_s, gx, wxq, slab_view(cache_mem_k[l]), slab_view(cache_mem_v[l]), wxo)

    y_sample = _ffn(x2_s.reshape(1, db, d), gffn, wg, wu, wd, gfin, db).reshape(db, ts, d)

    mem_shape = (depth, bsz, n_mem, mem_heads, mem_hd)
    return (y_prompt, y_sample,
            c_p[None], jnp.swapaxes(krt_p, 1, 2)[None], pst_p[None, :, POOL_HALO - POOL_STATE:],
            mk_p.reshape(mem_shape), mv_p.reshape(mem_shape),
            c_s.reshape(depth, db, ts, kv_lora), kr_s.reshape(depth, db, ts, rope),
            jnp.concatenate([cache_pool[l][:, 1:], up_s[:, None]], 1)[None])
```

```python
import functools

import jax
import jax.numpy as jnp
import numpy as np
from jax import lax
from jax.experimental import pallas as pl
from jax.experimental.pallas import tpu as pltpu

F32 = jnp.float32
BF16 = jnp.bfloat16

EPS = 1e-6
ROPE_THETA = 10000.0
PAGE_SIZE = 128
POOL_WINDOWS = (2, 4, 8, 16)
POOL_STATE = max(POOL_WINDOWS) - 1
POOL_HALO = 16
LANES = 128
VMEM_LIMIT = 56 * 1024 * 1024
ATTN_ROWS = 128
ATTN_AHEAD = 6


def _rms(x, g):
    return x * lax.rsqrt(jnp.mean(x * x, -1, keepdims=True) + EPS) * g


def _dot(a, b):
    return jnp.dot(a, b, preferred_element_type=F32)


def _dot_nt(a, b):
    return lax.dot_general(a, b, (((1,), (1,)), ((), ())), preferred_element_type=F32)


def _lane_tile(x, n):
    return jnp.concatenate([x] * n, -1)


def _const_spec(shape):
    nd = len(shape)
    return pl.BlockSpec(shape, lambda *_: (0,) * nd, pipeline_mode=pl.Buffered(1))


def _whole_spec(shape):
    nd = len(shape)
    return pl.BlockSpec(shape, lambda *_: (0,) * nd)


def _params(n_axes):
    return pltpu.CompilerParams(dimension_semantics=("arbitrary",) * n_axes,
                                vmem_limit_bytes=VMEM_LIMIT)


def _mla_inputs(x, gmix, win, gq, wuq, gkv, wuk, cos, sin, scale, dims):
    q_lora, kv_lora, pool_w, heads, nope, rope = dims
    h = _rms(x, gmix).astype(BF16)
    u = _dot(h, win)
    i1, i2, i3 = q_lora, q_lora + kv_lora, q_lora + kv_lora + pool_w
    cq, ckv, up, krs = u[:, :i1], u[:, i1:i2], u[:, i2:i3], u[:, i3:]
    q = _dot(_rms(cq, gq).astype(BF16), wuq)
    n0 = heads * nope
    n1 = n0 + heads * rope
    q_rope = (q[:, n0:n1] * cos + q[:, n1:] * sin) * scale
    q_lat = [(_dot(q[:, hh * nope:(hh + 1) * nope].astype(BF16), wuk[hh]) * scale).astype(BF16)
             for hh in range(heads)]
    q_rope = [q_rope[:, hh * rope:(hh + 1) * rope].astype(BF16) for hh in range(heads)]
    c = _rms(ckv, gkv)
    k_rope = krs * cos[:, :2 * rope] + pltpu.roll(krs, rope, 1) * sin[:, :2 * rope]
    return q_lat, q_rope, c, k_rope, up


def _pool_project(sums, up, cnt, wpool, pscale):
    cg = wpool.shape[-1]
    ys = []
    for g in range(len(POOL_WINDOWS)):
        m = sums[g] / cnt[g] - up[:, g * cg:(g + 1) * cg]
        ys.append(_dot(m.astype(BF16), wpool[g]))
    return jnp.concatenate(ys, -1) * pscale


def _front_prompt_kernel(x_ref, gmix_ref, win_ref, gq_ref, wuq_ref, gkv_ref, wuk_ref, cos_ref, sin_ref,
                         wpool_ref, pscale_ref,
                         qlat_ref, qrope_ref, c_ref, krt_ref, cbf_ref, ctbf_ref, krtbf_ref, py_ref, pst_ref,
                         prev_ref, *, scale, dims, tm):
    t = pl.program_id(1)
    heads, rope = dims[3], dims[5]
    q_lat, q_rope, c, k_rope, up = _mla_inputs(
        x_ref[0], gmix_ref[...], win_ref[...], gq_ref[...], wuq_ref[...], gkv_ref[...], wuk_ref,
        cos_ref[...], sin_ref[...], scale, dims)
    for hh in range(heads):
        qlat_ref[0, hh] = q_lat[hh]
        qrope_ref[0, hh] = q_rope[hh]
    c_ref[0] = c
    cbf_ref[0] = c.astype(BF16)
    ctbf_ref[0] = c.T.astype(BF16)
    krt = k_rope.T[:rope]
    krt_ref[0] = krt
    krtbf_ref[0] = krt.astype(BF16)

    @pl.when(t == 0)
    def _():
        prev_ref[...] = jnp.zeros_like(prev_ref)

    cg = wpool_ref.shape[-1]
    e = jnp.concatenate([prev_ref[...], up], 0)
    tail = up[tm - POOL_HALO:, :]
    prev_ref[...] = tail
    pst_ref[0] = tail
    sums = []
    s = e
    for g, w in enumerate(POOL_WINDOWS):
        s = s[:, (cg if g else 0):]
        s = s + pltpu.roll(s, w // 2, 0)
        sums.append(s[POOL_HALO:, :cg])
    pos1 = t * tm + lax.broadcasted_iota(jnp.int32, (tm, cg), 0) + 1
    cnt = [jnp.minimum(w, pos1).astype(F32) for w in POOL_WINDOWS]
    py_ref[0] = _pool_project(sums, up, cnt, wpool_ref, pscale_ref[...]).astype(BF16)


def _front_sample_kernel(x_ref, gmix_ref, win_ref, gq_ref, wuq_ref, gkv_ref, wuk_ref, cos_ref, sin_ref,
                         wpool_ref, pscale_ref, prev_ref,
                         qlat_ref, qrope_ref, c_ref, kr_ref, py_ref, up_ref, *, scale, dims, past):
    heads, rope = dims[3], dims[5]
    q_lat, q_rope, c, k_rope, up = _mla_inputs(
        x_ref[...], gmix_ref[...], win_ref[...], gq_ref[...], wuq_ref[...], gkv_ref[...], wuk_ref,
        cos_ref[...], sin_ref[...], scale, dims)
    for hh in range(heads):
        qlat_ref[hh] = q_lat[hh]
        qrope_ref[hh] = q_rope[hh]
    c_ref[...] = c
    kr_ref[...] = k_rope[:, :rope]
    up_ref[...] = up

    cg = wpool_ref.shape[-1]
    sums, cnt = [], []
    for g, w in enumerate(POOL_WINDOWS):
        s = up[:, g * cg:(g + 1) * cg]
        for j in range(1, w):
            s = s + prev_ref[POOL_STATE - j][:, g * cg:(g + 1) * cg]
        sums.append(s)
        cnt.append(float(min(w, past + 1)))
    py_ref[...] = _pool_project(sums, up, cnt, wpool_ref, pscale_ref[...]).astype(BF16)


def _mix_out(o_lat_heads, py, x, wuv_ref, wout_ref):
    o_mla = jnp.concatenate(
        [_dot(o.astype(BF16), wuv_ref[hh]) for hh, o in enumerate(o_lat_heads)], -1).astype(BF16)
    wm = o_mla.shape[-1]
    return x + _dot(o_mla, wout_ref[:wm, :]) + _dot(py, wout_ref[wm:, :])


def _paged_attention(ql, qr, cn, krn, cbf, rot_bf):
    cnr = cn.astype(BF16).astype(F32)
    krnr = krn.astype(BF16).astype(F32)

    def scores():
        return _dot_nt(ql, cbf[...]) + _dot(qr, rot_bf)

    def finish(s):
        s_new = (jnp.sum(ql.astype(F32) * cnr, -1, keepdims=True)
                 + jnp.sum(qr.astype(F32) * krnr, -1, keepdims=True))
        m = jnp.maximum(jnp.max(s, -1, keepdims=True), s_new)
        p = jnp.exp(s - m)
        p_new = jnp.exp(s_new - m)
        l = jnp.sum(p, -1, keepdims=True) + p_new
        acc = _dot(p.astype(BF16), cbf[...]) + p_new.astype(BF16).astype(F32) * cnr
        return acc / l

    return scores, finish


def _attn_prompt_kernel(tab_ref, pt_ref,
                        ql_ref, qr_ref, kt_ref, rt_ref, v_ref, x_ref, py_ref, wuv_ref, wout_ref,
                        sql_ref, sqr_ref, cn_ref, krn_ref, cache_c, cache_rt,
                        o_ref, so_ref,
                        m_ref, l_ref, acc_ref, kbuf, rbuf, cbf, sem,
                        *, tq, heads, n_pages, n_req, n_steps):
    step = pl.program_id(0)
    i = tab_ref[n_steps + step]
    j = tab_ref[2 * n_steps + step]
    req = tab_ref[3 * n_steps + step]
    tk = tq
    chains = [(hh, r) for hh in range(heads) for r in range(tq // ATTN_ROWS)]

    def page_copies(rq, sl):
        lat, rot = [], []
        for p in range(n_pages):
            page = pt_ref[rq * n_pages + p]
            toks = pl.ds(p * PAGE_SIZE, PAGE_SIZE)
            lat.append(pltpu.make_async_copy(cache_c.at[page], kbuf.at[sl, toks, :], sem.at[0, sl]))
            rot.append(pltpu.make_async_copy(cache_rt.at[page], rbuf.at[sl, :, toks], sem.at[1, sl]))
        return lat + rot

    @pl.when(step == 0)
    def _():
        for cp in page_copies(0, 0):
            cp.start()

    @pl.when(j == 0)
    def _():
        m_ref[...] = jnp.full_like(m_ref, -jnp.inf)
        l_ref[...] = jnp.zeros_like(l_ref)
        acc_ref[...] = jnp.zeros_like(acc_ref)

    def attend(diagonal, with_sample):
        if with_sample:
            sl = req % 2
            for cp in page_copies(req, sl):
                cp.wait()

            @pl.when(req + 1 < n_req)
            def _():
                for cp in page_copies(req + 1, 1 - sl):
                    cp.start()

            cbf[...] = kbuf[sl].astype(BF16)
            sample_scores, sample_finish = _paged_attention(
                sql_ref[0], sqr_ref[0], cn_ref[0], krn_ref[0], cbf, rbuf[sl].astype(BF16))

        kt, rt, v = kt_ref[0], rt_ref[0], v_ref[0]

        def scores(ci):
            hh, r = chains[ci]
            rows = pl.ds(r * ATTN_ROWS, ATTN_ROWS)
            return _dot(ql_ref[0, hh, rows, :], kt) + _dot(qr_ref[0, hh, rows, :], rt)

        pending = [scores(ci) for ci in range(ATTN_AHEAD)]
        for ci, (hh, r) in enumerate(chains):
            if ci + ATTN_AHEAD < len(chains):
                pending.append(scores(ci + ATTN_AHEAD))
            if with_sample and ci == 1:
                s_sample = sample_scores()
            if with_sample and ci == len(chains) - ATTN_AHEAD:
                so_ref[0] = sample_finish(s_sample)
            s = pending[ci]
            if diagonal:
                qpos = r * ATTN_ROWS + lax.broadcasted_iota(jnp.int32, s.shape, 0)
                kpos = lax.broadcasted_iota(jnp.int32, s.shape, 1)
                s = jnp.where(kpos <= qpos, s, -jnp.inf)
            m_prev = m_ref[ci]
            m_new = jnp.maximum(m_prev, jnp.max(s, -1, keepdims=True))
            alpha = jnp.exp(m_prev - m_new)
            p = jnp.exp(s - _lane_tile(m_new, tk // LANES))
            l_ref[ci] = alpha * l_ref[ci] + jnp.sum(p, -1, keepdims=True)
            acc_ref[ci] = (_lane_tile(alpha, acc_ref.shape[-1] // LANES) * acc_ref[ci]
                           + _dot(p.astype(BF16), v))
            m_ref[ci] = m_new

    for diagonal in (False, True):
        for with_sample in (False, True):
            on_diag = (j == i) if diagonal else (j < i)
            hosting = (req >= 0) if with_sample else (req < 0)
            pl.when(on_diag & hosting)(functools.partial(attend, diagonal, with_sample))

    @pl.when(j == i)
    def _():
        nr = tq // ATTN_ROWS
        rep = acc_ref.shape[-1] // LANES
        o_heads = [jnp.concatenate([acc_ref[hh * nr + r] / _lane_tile(l_ref[hh * nr + r], rep)
                                    for r in range(nr)], 0)
                   for hh in range(heads)]
        o_ref[0] = _mix_out(o_heads, py_ref[0], x_ref[0], wuv_ref, wout_ref)


def _mix_out_kernel(ol_ref, py_ref, x_ref, wuv_ref, wout_ref, o_ref, *, heads):
    o_ref[...] = _mix_out([ol_ref[hh] for hh in range(heads)], py_ref[...], x_ref[...],
                          wuv_ref, wout_ref)


def _mem_kv_kernel(mem_ref, g_ref, wk_ref, wv_ref, k_ref, v_ref, kbf_ref, vbf_ref):
    m = _rms(mem_ref[0], g_ref[...]).astype(BF16)
    k = _dot(m, wk_ref[...])
    v = _dot(m, wv_ref[...])
    k_ref[0] = k
    v_ref[0] = v
    kbf_ref[0] = k.astype(BF16)
    vbf_ref[0] = v.astype(BF16)


def _mem_attn_prompt_kernel(x_ref, g_ref, wq_ref, k_ref, v_ref, wo_ref, o_ref, *, heads, scale):
    x = x_ref[0]
    q = (_dot(_rms(x, g_ref[...]).astype(BF16), wq_ref[...]) * scale).astype(BF16)
    hd = q.shape[-1] // heads
    outs = []
    for hh in range(heads):
        sl = slice(hh * hd, (hh + 1) * hd)
        s = _dot_nt(q[:, sl], k_ref[0, :, sl])
        p = jnp.exp(s - jnp.max(s, -1, keepdims=True))
        o = _dot(p.astype(BF16), v_ref[0, :, sl]) / jnp.sum(p, -1, keepdims=True)
        outs.append(o.astype(BF16))
    o_ref[0] = x + _dot(jnp.concatenate(outs, -1), wo_ref[...])


def _slab_col(r, heads, rr):
    return (r % heads) * (rr // heads) + r // heads


def _mem_query_slabs(x, g, wq_ref, q_ref, heads, scale):
    db = x.shape[0]
    rr = q_ref.shape[0] // db
    hq = _rms(x, g).astype(BF16)
    for r in range(rr):
        j = _slab_col(r, heads, rr)
        q_ref[pl.ds(r, db, stride=rr), :] = _dot(hq, wq_ref[:, j * LANES:(j + 1) * LANES]) * scale


def _mem_attend_request(q_ref, a_ref, k, v, request, heads, n_mem):
    rr = k.shape[0] // n_mem
    row = pl.multiple_of(request * rr, rr)
    qv = q_ref[pl.ds(row, rr), :]
    part = jnp.sum(k.reshape(n_mem, rr, LANES) * qv[None], -1, keepdims=True)
    s = part
    for d in range(1, rr // heads):
        s = s + pltpu.roll(part, d * heads, 1)
    p = jnp.exp(s - jnp.max(s, 0, keepdims=True))
    l = jnp.sum(p, 0)
    a_ref[pl.ds(row, rr), :] = jnp.sum(p * v.reshape(n_mem, rr, LANES), 0) / l


def _mem_output(x, a_ref, wo_ref, heads):
    db = x.shape[0]
    rr = a_ref.shape[0] // db
    acc = x
    for r in range(rr):
        j = _slab_col(r, heads, rr)
        a = a_ref[pl.ds(r, db, stride=rr), :].astype(BF16)
        acc = acc + _dot(a, wo_ref[j * LANES:(j + 1) * LANES, :])
    return acc


def _ffn_mem_kernel(x_ref, g_ref, wg_ref, wu_ref, wd_ref, gf_ref,
                    xs_ref, gx_ref, wq_ref, k_ref, v_ref, wo_ref,
                    y_ref, os_ref, q_ref, a_ref, *, heads, scale, nb, n_mem):
    s_id = pl.program_id(0)

    @pl.when(s_id == 0)
    def _():
        _mem_query_slabs(xs_ref[...], gx_ref[...], wq_ref, q_ref, heads, scale)

    x = x_ref[...]
    h = _rms(x, g_ref[...]).astype(BF16)
    gate = _dot(h, wg_ref[...])
    a = (gate * jax.nn.sigmoid(gate) * _dot(h, wu_ref[...])).astype(BF16)
    y_ref[...] = _rms(x + _dot(a, wd_ref[...]), gf_ref[...])
    for bb in range(nb):
        _mem_attend_request(q_ref, a_ref, k_ref[bb], v_ref[bb], s_id * nb + bb, heads, n_mem)

    @pl.when(s_id == pl.num_programs(0) - 1)
    def _():
        os_ref[...] = _mem_output(xs_ref[...], a_ref, wo_ref, heads)


def _ffn_kernel(x_ref, g_ref, wg_ref, wu_ref, wd_ref, gf_ref, o_ref):
    x = x_ref[0]
    h = _rms(x, g_ref[...]).astype(BF16)
    gate = _dot(h, wg_ref[...])
    a = (gate * jax.nn.sigmoid(gate) * _dot(h, wu_ref[...])).astype(BF16)
    o_ref[0] = _rms(x + _dot(a, wd_ref[...]), gf_ref[...])


def _rope_tables(pos, rope, heads):
    half = rope // 2
    inv = jnp.power(ROPE_THETA, -(jnp.arange(half, dtype=F32) / half))
    ang = pos.astype(F32)[:, None] * inv[None, :]
    cos, sin = jnp.cos(ang), jnp.sin(ang)
    return (jnp.tile(jnp.concatenate([cos, cos], -1), (1, heads)),
            jnp.tile(jnp.concatenate([-sin, sin], -1), (1, heads)))


def _swap_halves(w):
    half = w.shape[-1] // 2
    return jnp.concatenate([w[..., half:], w[..., :half]], -1)


def _row(v):
    return v.reshape(1, -1)


def _tile(n, pref):
    return pref if n % pref == 0 else n


def _ffn(x, g, wg, wu, wd, gf, tm):
    bsz, t, d = x.shape
    spec = pl.BlockSpec((1, tm, d), lambda b, i: (b, i, 0))
    return pl.pallas_call(
        _ffn_kernel,
        grid=(bsz, t // tm),
        in_specs=[spec, _const_spec(g.shape), _const_spec(wg.shape), _const_spec(wu.shape),
                  _const_spec(wd.shape), _const_spec(gf.shape)],
        out_specs=spec,
        out_shape=jax.ShapeDtypeStruct(x.shape, F32),
        compiler_params=_params(2),
        name="ffn",
    )(x, g, wg, wu, wd, gf)


def _attention_schedule(bsz, n_tiles, n_req):
    steps = [(b, i, j) for b in range(bsz) for i in range(n_tiles) for j in range(i + 1)]
    n_steps = len(steps)
    assert n_req <= n_steps, "at most one sample request per attention step"
    by_cost = sorted(range(n_steps), key=lambda s: (steps[s][1] == steps[s][2], steps[s][1]), reverse=True)
    idle = set(by_cost[:n_steps - n_req])
    assert 0 not in idle
    tab = np.zeros((5, n_steps), np.int32)
    nxt = 0
    for s, (b, i, j) in enumerate(steps):
        hosted = s not in idle
        tab[:, s] = (b, i, j, nxt if hosted else -1, nxt if hosted else nxt - 1)
        nxt += hosted
    return tab


def _attention_with_paged(qlat_p, qrope_p, ctbf, krtbf, cbf, x, py, wuv, wout,
                          page_table, qlat_s, qrope_s, c_new, kr_new, cache_c, cache_rt, tq):
    bsz, heads, seq, kv = qlat_p.shape
    rope = qrope_p.shape[-1]
    d = x.shape[-1]
    db = qlat_s.shape[0]
    n_pages = page_table.shape[1]
    past = n_pages * PAGE_SIZE
    assert seq % tq == 0 and tq % ATTN_ROWS == 0
    n_chain = heads * (tq // ATTN_ROWS)
    tab = _attention_schedule(bsz, seq // tq, db)
    n_steps = tab.shape[1]
    col = lambda r: (lambda s, tab_ref, pt: tab_ref[r * n_steps + s])
    bb, ii, jj, rb = col(0), col(1), col(2), col(4)
    q_spec = lambda w: pl.BlockSpec((1, heads, tq, w), lambda *a: (bb(*a), 0, ii(*a), 0))
    kt_spec = lambda w: pl.BlockSpec((1, w, tq), lambda *a: (bb(*a), 0, jj(*a)))
    v_spec = pl.BlockSpec((1, tq, kv), lambda *a: (bb(*a), jj(*a), 0))
    row_spec = lambda w: pl.BlockSpec((1, tq, w), lambda *a: (bb(*a), ii(*a), 0))
    req_spec = lambda h, w: pl.BlockSpec((1, h, w), lambda *a: (rb(*a), 0, 0))
    return pl.pallas_call(
        functools.partial(_attn_prompt_kernel, tq=tq, heads=heads, n_pages=n_pages, n_req=db, n_steps=n_steps),
        grid_spec=pltpu.PrefetchScalarGridSpec(
            num_scalar_prefetch=2,
            grid=(n_steps,),
            in_specs=[q_spec(kv), q_spec(rope), kt_spec(kv), kt_spec(rope), v_spec, row_spec(d),
                      row_spec(py.shape[-1]), _const_spec(wuv.shape), _const_spec(wout.shape),
                      req_spec(heads, kv), req_spec(heads, rope), req_spec(1, kv), req_spec(1, rope),
                      pl.BlockSpec(memory_space=pl.ANY), pl.BlockSpec(memory_space=pl.ANY)],
            out_specs=[row_spec(d), req_spec(heads, kv)],
            scratch_shapes=[pltpu.VMEM((n_chain, ATTN_ROWS, LANES), F32),
                            pltpu.VMEM((n_chain, ATTN_ROWS, LANES), F32),
                            pltpu.VMEM((n_chain, ATTN_ROWS, kv), F32),
                            pltpu.VMEM((2, past, kv), F32), pltpu.VMEM((2, rope, past), F32),
                            pltpu.VMEM((past, kv), BF16), pltpu.SemaphoreType.DMA((2, 2))],
        ),
        out_shape=[jax.ShapeDtypeStruct((bsz, seq, d), F32), jax.ShapeDtypeStruct((db, heads, kv), F32)],
        compiler_params=_params(1),
        name="attn_prompt_paged",
    )(jnp.asarray(tab.reshape(-1)), page_table.reshape(-1),
      qlat_p, qrope_p, ctbf, krtbf, cbf, x, py, wuv, wout,
      qlat_s, qrope_s, c_new, kr_new, cache_c, cache_rt)


def _ffn_with_mem_attention(x, g, wg, wu, wd, gf, xs, gx, wxq, mem_k, mem_v, wxo, tm, heads, scale, n_mem):
    rows, d = x.shape
    db = xs.shape[0]
    n_steps = rows // tm
    assert rows % tm == 0 and db % n_steps == 0
    nb = db // n_steps
    slab = mem_k.shape[1] // n_mem
    x_spec = pl.BlockSpec((tm, d), lambda s: (s, 0))
    cache_spec = pl.BlockSpec((nb,) + mem_k.shape[1:], lambda s: (s, 0, 0))
    return pl.pallas_call(
        functools.partial(_ffn_mem_kernel, heads=heads, scale=scale, nb=nb, n_mem=n_mem),
        grid=(n_steps,),
        in_specs=[x_spec] + [_const_spec(a.shape) for a in (g, wg, wu, wd, gf, xs, gx, wxq)]
        + [cache_spec, cache_spec, _const_spec(wxo.shape)],
        out_specs=[x_spec, _whole_spec((db, d))],
        out_shape=[jax.ShapeDtypeStruct((rows, d), F32), jax.ShapeDtypeStruct((db, d), F32)],
        scratch_shapes=[pltpu.VMEM((db * slab, LANES), F32), pltpu.VMEM((db * slab, LANES), F32)],
        compiler_params=_params(1),
        name="ffn_mem",
    )(x, g, wg, wu, wd, gf, xs, gx, wxq, mem_k, mem_v, wxo)


def kernel(x_prompt, x_sample, mem_prompt, cache_kv_latent, cache_k_rope, cache_pool, cache_mem_k, cache_mem_v, page_table, norm_mix, w_in, q_norm, w_uq, kv_norm, w_uk, w_uv, w_pool, pool_scale, w_out, norm_x, mem_norm, w_xq, w_mk, w_mv, w_xo, norm_ffn, w_gate, w_up, w_down, norm_final):
    depth = w_in.shape[0]
    assert depth == 1, "single-layer trunk"
    bsz, seq, d = x_prompt.shape
    db, ts, _ = x_sample.shape
    assert ts == 1, "one new token per request"
    q_lora, kv_lora = q_norm.shape[1], kv_norm.shape[1]
    heads, nope = w_uk.shape[2], w_uk.shape[3]
    rope = cache_k_rope.shape[-1]
    pool_w = pool_scale.shape[1]
    v_head = w_uv.shape[3]
    n_mem, mem_heads, mem_hd = cache_mem_k.shape[2], cache_mem_k.shape[3], cache_mem_k.shape[4]
    n_pages = page_table.shape[1]
    assert cache_kv_latent.shape[2] == PAGE_SIZE
    past = n_pages * PAGE_SIZE
    mla_scale = float((nope + rope) ** -0.5)
    mem_scale = float(mem_hd ** -0.5)
    dims = (q_lora, kv_lora, pool_w, heads, nope, rope)
    l = 0

    i1, i2, i3 = q_lora, q_lora + kv_lora, q_lora + kv_lora + rope
    wi = w_in[l]
    w_kr = wi[:, i2:i3]
    win = jnp.concatenate([wi[:, :i2], wi[:, i3:], w_kr, _swap_halves(w_kr)], -1).astype(BF16)
    wq3 = w_uq[l].reshape(q_lora, heads, nope + rope)
    wq_rope = wq3[:, :, nope:]
    wuq = jnp.concatenate([wq3[:, :, :nope].reshape(q_lora, heads * nope),
                           wq_rope.reshape(q_lora, heads * rope),
                           _swap_halves(wq_rope).reshape(q_lora, heads * rope)], -1).astype(BF16)
    wuk = jnp.transpose(w_uk[l], (1, 2, 0)).astype(BF16)
    wuv = jnp.transpose(w_uv[l], (1, 0, 2)).astype(BF16)
    wpool = w_pool[l].astype(BF16)
    wout = w_out[l].astype(BF16)
    wxq, wxo = w_xq[l].astype(BF16), w_xo[l].astype(BF16)
    wmk, wmv = w_mk[l].astype(BF16), w_mv[l].astype(BF16)
    wg, wu, wd = w_gate[l].astype(BF16), w_up[l].astype(BF16), w_down[l].astype(BF16)
    gmix, gq, gkv = _row(norm_mix[l]), _row(q_norm[l]), _row(kv_norm[l])
    gx, gmem, gffn, gfin = _row(norm_x[l]), _row(mem_norm[l]), _row(norm_ffn[l]), _row(norm_final)
    pscale = _row(pool_scale[l])
    front_consts = (gmix, win, gq, wuq, gkv, wuk)
    hr = heads * rope

    tm = _tile(seq, 512)
    cos_p, sin_p = _rope_tables(jnp.arange(seq), rope, heads)
    row_spec = lambda w: pl.BlockSpec((1, tm, w), lambda b, t: (b, t, 0))
    head_spec = lambda w: pl.BlockSpec((1, heads, tm, w), lambda b, t: (b, 0, t, 0))
    tab_spec = pl.BlockSpec((tm, hr), lambda b, t: (t, 0))
    col_spec = lambda w: pl.BlockSpec((1, w, tm), lambda b, t: (b, 0, t))
    qlat_p, qrope_p, c_p, krt_p, cbf_p, ctbf_p, krtbf_p, py_p, pst_p = pl.pallas_call(
        functools.partial(_front_prompt_kernel, scale=mla_scale, dims=dims, tm=tm),
        grid=(bsz, seq // tm),
        in_specs=[row_spec(d)] + [_const_spec(a.shape) for a in front_consts]
        + [tab_spec, tab_spec, _const_spec(wpool.shape), _const_spec(pscale.shape)],
        out_specs=[head_spec(kv_lora), head_spec(rope), row_spec(kv_lora), col_spec(rope),
                   row_spec(kv_lora), col_spec(kv_lora), col_spec(rope), row_spec(pool_w),
                   pl.BlockSpec((1, POOL_HALO, pool_w), lambda b, t: (b, 0, 0))],
        out_shape=[jax.ShapeDtypeStruct((bsz, heads, seq, kv_lora), BF16),
                   jax.ShapeDtypeStruct((bsz, heads, seq, rope), BF16),
                   jax.ShapeDtypeStruct((bsz, seq, kv_lora), F32),
                   jax.ShapeDtypeStruct((bsz, rope, seq), F32),
                   jax.ShapeDtypeStruct((bsz, seq, kv_lora), BF16),
                   jax.ShapeDtypeStruct((bsz, kv_lora, seq), BF16),
                   jax.ShapeDtypeStruct((bsz, rope, seq), BF16),
                   jax.ShapeDtypeStruct((bsz, seq, pool_w), BF16),
                   jax.ShapeDtypeStruct((bsz, POOL_HALO, pool_w), F32)],
        scratch_shapes=[pltpu.VMEM((POOL_HALO, pool_w), F32)],
        compiler_params=_params(2),
        name="front_prompt",
    )(x_prompt, *front_consts, cos_p, sin_p, wpool, pscale)

    xs = x_sample.reshape(db, d)
    cos_s, sin_s = _rope_tables(jnp.full((db,), past), rope, heads)
    prev_s = jnp.transpose(cache_pool[l], (1, 0, 2))
    sample_in = (xs, *front_consts, cos_s, sin_s, wpool, pscale, prev_s)
    qlat_s, qrope_s, c_s, kr_s, py_s, up_s = pl.pallas_call(
        functools.partial(_front_sample_kernel, scale=mla_scale, dims=dims, past=past),
        grid=(1,),
        in_specs=[_const_spec(a.shape) for a in sample_in],
        out_specs=[_whole_spec(s) for s in ((heads, db, kv_lora), (heads, db, rope), (db, kv_lora),
                                            (db, rope), (db, pool_w), (db, pool_w))],
        out_shape=[jax.ShapeDtypeStruct((heads, db, kv_lora), BF16),
                   jax.ShapeDtypeStruct((heads, db, rope), BF16),
                   jax.ShapeDtypeStruct((db, kv_lora), F32),
                   jax.ShapeDtypeStruct((db, rope), F32),
                   jax.ShapeDtypeStruct((db, pool_w), BF16),
                   jax.ShapeDtypeStruct((db, pool_w), F32)],
        compiler_params=_params(1),
        name="front_sample",
    )(*sample_in)

    x1_p, olat_s = _attention_with_paged(
        qlat_p, qrope_p, ctbf_p, krtbf_p, cbf_p, x_prompt, py_p, wuv, wout,
        page_table, jnp.swapaxes(qlat_s, 0, 1), jnp.swapaxes(qrope_s, 0, 1),
        c_s.reshape(db, 1, kv_lora), kr_s.reshape(db, 1, rope),
        cache_kv_latent[l], jnp.swapaxes(cache_k_rope[l], 1, 2),
        tq=_tile(seq, 512))

    mem_spec = pl.BlockSpec((1, n_mem, d), lambda b: (b, 0, 0))
    mk_p, mv_p, mkbf_p, mvbf_p = pl.pallas_call(
        _mem_kv_kernel,
        grid=(bsz,),
        in_specs=[mem_spec, _const_spec(gmem.shape), _const_spec(wmk.shape), _const_spec(wmv.shape)],
        out_specs=[mem_spec] * 4,
        out_shape=[jax.ShapeDtypeStruct((bsz, n_mem, d), F32)] * 2
        + [jax.ShapeDtypeStruct((bsz, n_mem, d), BF16)] * 2,
        compiler_params=_params(1),
        name="mem_kv",
    )(mem_prompt, gmem, wmk, wmv)

    tmem = _tile(seq, 512)
    xm_spec = pl.BlockSpec((1, tmem, d), lambda b, i: (b, i, 0))
    memkv_spec = pl.BlockSpec((1, n_mem, d), lambda b, i: (b, 0, 0))
    x2_p = pl.pallas_call(
        functools.partial(_mem_attn_prompt_kernel, heads=mem_heads, scale=mem_scale),
        grid=(bsz, seq // tmem),
        in_specs=[xm_spec, _const_spec(gx.shape), _const_spec(wxq.shape), memkv_spec, memkv_spec,
                  _const_spec(wxo.shape)],
        out_specs=xm_spec,
        out_shape=jax.ShapeDtypeStruct((bsz, seq, d), F32),
        compiler_params=_params(2),
        name="mem_attn_prompt",
    )(x1_p, gx, wxq, mkbf_p, mvbf_p, wxo)

    mix_in = (jnp.swapaxes(olat_s, 0, 1), py_s, xs, wuv, wout)
    x1_s = pl.pallas_call(
        functools.partial(_mix_out_kernel, heads=heads),
        grid=(1,),
        in_specs=[_const_spec(a.shape) for a in mix_in],
        out_specs=_whole_spec((db, d)),
        out_shape=jax.ShapeDtypeStruct((db, d), F32),
        compiler_params=_params(1),
        name="mix_out_sample",
    )(*mix_in)

    assert mem_hd % LANES == 0
    slab = mem_heads * (mem_hd // LANES)

    def slab_view(a):
        a = a.reshape(db, n_mem, mem_heads, mem_hd // LANES, LANES)
        return jnp.transpose(a, (0, 1, 3, 2, 4)).reshape(db, n_mem * slab, LANES)

    y_prompt, x2_s = _ffn_with_mem_attention(
        x2_p.reshape(bsz * seq, d), gffn, wg, wu, wd, gfin,
        x1_s, gx, wxq, slab_view(cache_mem_k[l]), slab_view(cache_mem_v[l]), wxo,
        tm=_tile(bsz * seq, 256), heads=mem_heads, scale=mem_scale, n_mem=n_mem)
    y_prompt = y_prompt.reshape(bsz, seq, d)

    y_sample = _ffn(x2_s.reshape(1, db, d), gffn, wg, wu, wd, gfin, db).reshape(db, ts, d)

    mem_shape = (depth, bsz, n_mem, mem_heads, mem_hd)
    return (y_prompt, y_sample,
            c_p[None], jnp.swapaxes(krt_p, 1, 2)[None], pst_p[None, :, POOL_HALO - POOL_STATE:],
            mk_p.reshape(mem_shape), mv_p.reshape(mem_shape),
            c_s.reshape(depth, db, ts, kv_lora), kr_s.reshape(depth, db, ts, rope),
            jnp.concatenate([cache_pool[l][:, 1:], up_s[:, None]], 1)[None])
```

```python
import functools

import jax
import jax.numpy as jnp
import numpy as np
from jax import lax
from jax.experimental import pallas as pl
from jax.experimental.pallas import tpu as pltpu

F32 = jnp.float32
BF16 = jnp.bfloat16

EPS = 1e-6
ROPE_THETA = 10000.0
PAGE_SIZE = 128
POOL_WINDOWS = (2, 4, 8, 16)
POOL_STATE = max(POOL_WINDOWS) - 1
POOL_HALO = 16
LANES = 128
VMEM_LIMIT = 56 * 1024 * 1024
ATTN_ROWS = 256
ATTN_AHEAD = 3


def _rms(x, g):
    return x * lax.rsqrt(jnp.mean(x * x, -1, keepdims=True) + EPS) * g


def _dot(a, b):
    return jnp.dot(a, b, preferred_element_type=F32)


def _dot_nt(a, b):
    return lax.dot_general(a, b, (((1,), (1,)), ((), ())), preferred_element_type=F32)


def _lane_tile(x, n):
    return jnp.concatenate([x] * n, -1)


def _const_spec(shape):
    nd = len(shape)
    return pl.BlockSpec(shape, lambda *_: (0,) * nd, pipeline_mode=pl.Buffered(1))


def _whole_spec(shape):
    nd = len(shape)
    return pl.BlockSpec(shape, lambda *_: (0,) * nd)


def _params(n_axes):
    return pltpu.CompilerParams(dimension_semantics=("arbitrary",) * n_axes,
                                vmem_limit_bytes=VMEM_LIMIT)


def _mla_inputs(x, gmix, win, gq, wuq, gkv, wuk, cos, sin, scale, dims):
    q_lora, kv_lora, pool_w, heads, nope, rope = dims
    h = _rms(x, gmix).astype(BF16)
    u = _dot(h, win)
    i1, i2, i3 = q_lora, q_lora + kv_lora, q_lora + kv_lora + pool_w
    cq, ckv, up, krs = u[:, :i1], u[:, i1:i2], u[:, i2:i3], u[:, i3:]
    q = _dot(_rms(cq, gq).astype(BF16), wuq)
    n0 = heads * nope
    n1 = n0 + heads * rope
    q_rope = (q[:, n0:n1] * cos + q[:, n1:] * sin) * scale
    q_lat = [(_dot(q[:, hh * nope:(hh + 1) * nope].astype(BF16), wuk[hh]) * scale).astype(BF16)
             for hh in range(heads)]
    q_rope = [q_rope[:, hh * rope:(hh + 1) * rope].astype(BF16) for hh in range(heads)]
    c = _rms(ckv, gkv)
    k_rope = krs * cos[:, :2 * rope] + pltpu.roll(krs, rope, 1) * sin[:, :2 * rope]
    return q_lat, q_rope, c, k_rope, up


def _pool_project(sums, up, cnt, wpool, pscale):
    cg = wpool.shape[-1]
    ys = []
    for g in range(len(POOL_WINDOWS)):
        m = sums[g] / cnt[g] - up[:, g * cg:(g + 1) * cg]
        ys.append(_dot(m.astype(BF16), wpool[g]))
    return jnp.concatenate(ys, -1) * pscale


def _front_prompt_kernel(x_ref, gmix_ref, win_ref, gq_ref, wuq_ref, gkv_ref, wuk_ref, cos_ref, sin_ref,
                         wpool_ref, pscale_ref,
                         qlat_ref, qrope_ref, c_ref, krt_ref, cbf_ref, ctbf_ref, krtbf_ref, py_ref, pst_ref,
                         prev_ref, *, scale, dims, tm):
    t = pl.program_id(1)
    heads, rope = dims[3], dims[5]
    q_lat, q_rope, c, k_rope, up = _mla_inputs(
        x_ref[0], gmix_ref[...], win_ref[...], gq_ref[...], wuq_ref[...], gkv_ref[...], wuk_ref,
        cos_ref[...], sin_ref[...], scale, dims)
    for hh in range(heads):
        qlat_ref[0, hh] = q_lat[hh]
        qrope_ref[0, hh] = q_rope[hh]
    c_ref[0] = c
    cbf_ref[0] = c.astype(BF16)
    ctbf_ref[0] = c.T.astype(BF16)
    krt = k_rope.T[:rope]
    krt_ref[0] = krt
    krtbf_ref[0] = krt.astype(BF16)

    @pl.when(t == 0)
    def _():
        prev_ref[...] = jnp.zeros_like(prev_ref)

    cg = wpool_ref.shape[-1]
    e = jnp.concatenate([prev_ref[...], up], 0)
    tail = up[tm - POOL_HALO:, :]
    prev_ref[...] = tail
    pst_ref[0] = tail
    sums = []
    s = e
    for g, w in enumerate(POOL_WINDOWS):
        s = s[:, (cg if g else 0):]
        s = s + pltpu.roll(s, w // 2, 0)
        sums.append(s[POOL_HALO:, :cg])
    pos1 = t * tm + lax.broadcasted_iota(jnp.int32, (tm, cg), 0) + 1
    cnt = [jnp.minimum(w, pos1).astype(F32) for w in POOL_WINDOWS]
    py_ref[0] = _pool_project(sums, up, cnt, wpool_ref, pscale_ref[...]).astype(BF16)


def _front_sample_kernel(x_ref, gmix_ref, win_ref, gq_ref, wuq_ref, gkv_ref, wuk_ref, cos_ref, sin_ref,
                         wpool_ref, pscale_ref, prev_ref,
                         qlat_ref, qrope_ref, c_ref, kr_ref, py_ref, up_ref, *, scale, dims, past):
    heads, rope = dims[3], dims[5]
    q_lat, q_rope, c, k_rope, up = _mla_inputs(
        x_ref[...], gmix_ref[...], win_ref[...], gq_ref[...], wuq_ref[...], gkv_ref[...], wuk_ref,
        cos_ref[...], sin_ref[...], scale, dims)
    for hh in range(heads):
        qlat_ref[hh] = q_lat[hh]
        qrope_ref[hh] = q_rope[hh]
    c_ref[...] = c
    kr_ref[...] = k_rope[:, :rope]
    up_ref[...] = up

    cg = wpool_ref.shape[-1]
    sums, cnt = [], []
    for g, w in enumerate(POOL_WINDOWS):
        s = up[:, g * cg:(g + 1) * cg]
        for j in range(1, w):
            s = s + prev_ref[POOL_STATE - j][:, g * cg:(g + 1) * cg]
        sums.append(s)
        cnt.append(float(min(w, past + 1)))
    py_ref[...] = _pool_project(sums, up, cnt, wpool_ref, pscale_ref[...]).astype(BF16)


def _mix_out(o_lat_heads, py, x, wuv_ref, wout_ref):
    o_mla = jnp.concatenate(
        [_dot(o.astype(BF16), wuv_ref[hh]) for hh, o in enumerate(o_lat_heads)], -1).astype(BF16)
    wm = o_mla.shape[-1]
    return x + _dot(o_mla, wout_ref[:wm, :]) + _dot(py, wout_ref[wm:, :])


def _paged_attention(ql, qr, cn, krn, cbf, rot, pieces):
    cnr = cn.astype(BF16).astype(F32)
    krnr = krn.astype(BF16).astype(F32)
    w = cbf.shape[0] // pieces

    def scores(k):
        return (_dot_nt(ql, cbf[k * w:(k + 1) * w, :])
                + _dot(qr, rot[:, k * w:(k + 1) * w].astype(BF16)))

    def softmax(s_parts):
        s = jnp.concatenate(s_parts, -1)
        s_new = (jnp.sum(ql.astype(F32) * cnr, -1, keepdims=True)
                 + jnp.sum(qr.astype(F32) * krnr, -1, keepdims=True))
        m = jnp.maximum(jnp.max(s, -1, keepdims=True), s_new)
        p = jnp.exp(s - m)
        p_new = jnp.exp(s_new - m)
        return p.astype(BF16), jnp.sum(p, -1, keepdims=True) + p_new, p_new

    def values(p, k):
        return _dot(p[:, k * w:(k + 1) * w], cbf[k * w:(k + 1) * w, :])

    def finish(v_parts, l, p_new):
        return (sum(v_parts) + p_new.astype(BF16).astype(F32) * cnr) / l

    return scores, softmax, values, finish


def _attn_prompt_kernel(tab_ref, pt_ref,
                        ql_ref, qr_ref, kt_ref, rt_ref, v_ref, x_ref, py_ref, wuv_ref, wout_ref,
                        sql_ref, sqr_ref, cn_ref, krn_ref, cache_c, cache_rt,
                        o_ref, so_ref,
                        m_ref, l_ref, acc_ref, kbuf, rbuf, cbf, sem,
                        *, tq, heads, n_pages, n_req, n_steps):
    step = pl.program_id(0)
    i = tab_ref[n_steps + step]
    j = tab_ref[2 * n_steps + step]
    req = tab_ref[3 * n_steps + step]
    tk = tq
    chains = [(hh, r) for hh in range(heads) for r in range(tq // ATTN_ROWS)]

    def page_copies(rq, sl):
        lat, rot = [], []
        for p in range(n_pages):
            page = pt_ref[rq * n_pages + p]
            toks = pl.ds(p * PAGE_SIZE, PAGE_SIZE)
            lat.append(pltpu.make_async_copy(cache_c.at[page], kbuf.at[sl, toks, :], sem.at[0, sl]))
            rot.append(pltpu.make_async_copy(cache_rt.at[page], rbuf.at[sl, :, toks], sem.at[1, sl]))
        return lat + rot

    @pl.when(step == 0)
    def _():
        for cp in page_copies(0, 0):
            cp.start()

    @pl.when(j == 0)
    def _():
        m_ref[...] = jnp.full_like(m_ref, -jnp.inf)
        l_ref[...] = jnp.zeros_like(l_ref)
        acc_ref[...] = jnp.zeros_like(acc_ref)

    def attend(diagonal, with_sample):
        if with_sample:
            sl = req % 2
            for cp in page_copies(req, sl):
                cp.wait()
            nxt = jnp.minimum(req + 1, n_req - 1)
            for cp in page_copies(nxt, 1 - sl):
                cp.start()

            cbf[...] = kbuf[sl].astype(BF16)
            pieces = len(chains) // 2
            s_scores, s_softmax, s_values, s_finish = _paged_attention(
                sql_ref[0], sqr_ref[0], cn_ref[0], krn_ref[0], cbf, rbuf.at[sl], pieces)
            s_parts, v_parts = [], []

        kt, rt, v = kt_ref[0], rt_ref[0], v_ref[0]

        def width(r):
            return (r + 1) * ATTN_ROWS if diagonal else tk

        def scores(ci):
            hh, r = chains[ci]
            rows = pl.ds(r * ATTN_ROWS, ATTN_ROWS)
            w = width(r)
            return _dot(ql_ref[0, hh, rows, :], kt[:, :w]) + _dot(qr_ref[0, hh, rows, :], rt[:, :w])

        pending = [scores(ci) for ci in range(ATTN_AHEAD)]
        for ci, (hh, r) in enumerate(chains):
            if ci + ATTN_AHEAD < len(chains):
                pending.append(scores(ci + ATTN_AHEAD))
            if with_sample and ci < pieces:
                s_parts.append(s_scores(ci))
            if with_sample and ci == pieces:
                p_sample, l_sample, p_new = s_softmax(s_parts)
            if with_sample and ci >= pieces:
                v_parts.append(s_values(p_sample, ci - pieces))
            s = pending[ci]
            if diagonal:
                qpos = r * ATTN_ROWS + lax.broadcasted_iota(jnp.int32, s.shape, 0)
                kpos = lax.broadcasted_iota(jnp.int32, s.shape, 1)
                s = jnp.where(kpos <= qpos, s, -jnp.inf)
            m_prev = m_ref[ci]
            m_new = jnp.maximum(m_prev, jnp.max(s, -1, keepdims=True))
            alpha = jnp.exp(m_prev - m_new)
            p = jnp.exp(s - _lane_tile(m_new, width(r) // LANES))
            l_ref[ci] = alpha * l_ref[ci] + jnp.sum(p, -1, keepdims=True)
            acc_ref[ci] = (_lane_tile(alpha, acc_ref.shape[-1] // LANES) * acc_ref[ci]
                           + _dot(p.astype(BF16), v[:width(r)]))
            m_ref[ci] = m_new
        if with_sample:
            so_ref[0] = s_finish(v_parts, l_sample, p_new)

            @pl.when(req == n_req - 1)
            def _():
                for cp in page_copies(nxt, 1 - sl):
                    cp.wait()

    for diagonal in (False, True):
        for with_sample in (False, True):
            on_diag = (j == i) if diagonal else (j < i)
            hosting = (req >= 0) if with_sample else (req < 0)
            pl.when(on_diag & hosting)(functools.partial(attend, diagonal, with_sample))

    @pl.when(j == i)
    def _():
        nr = tq // ATTN_ROWS
        rep = acc_ref.shape[-1] // LANES
        o_heads = [jnp.concatenate([acc_ref[hh * nr + r] / _lane_tile(l_ref[hh * nr + r], rep)
                                    for r in range(nr)], 0)
                   for hh in range(heads)]
        o_ref[0] = _mix_out(o_heads, py_ref[0], x_ref[0], wuv_ref, wout_ref)


def _mix_out_kernel(ol_ref, py_ref, x_ref, wuv_ref, wout_ref, o_ref, *, heads):
    o_ref[...] = _mix_out([ol_ref[hh] for hh in range(heads)], py_ref[...], x_ref[...],
                          wuv_ref, wout_ref)


def _mem_kv_kernel(mem_ref, g_ref, wk_ref, wv_ref, k_ref, v_ref, kbf_ref, vbf_ref):
    m = _rms(mem_ref[0], g_ref[...]).astype(BF16)
    k = _dot(m, wk_ref[...])
    v = _dot(m, wv_ref[...])
    k_ref[0] = k
    v_ref[0] = v
    kbf_ref[0] = k.astype(BF16)
    vbf_ref[0] = v.astype(BF16)


def _mem_attn_prompt_kernel(x_ref, g_ref, wq_ref, k_ref, v_ref, wo_ref, o_ref, *, heads, scale):
    x = x_ref[0]
    q = (_dot(_rms(x, g_ref[...]).astype(BF16), wq_ref[...]) * scale).astype(BF16)
    hd = q.shape[-1] // heads
    outs = []
    for hh in range(heads):
        sl = slice(hh * hd, (hh + 1) * hd)
        s = _dot_nt(q[:, sl], k_ref[0, :, sl])
        p = jnp.exp(s - jnp.max(s, -1, keepdims=True))
        o = _dot(p.astype(BF16), v_ref[0, :, sl]) / jnp.sum(p, -1, keepdims=True)
        outs.append(o.astype(BF16))
    o_ref[0] = x + _dot(jnp.concatenate(outs, -1), wo_ref[...])


def _slab_col(r, heads, rr):
    return (r % heads) * (rr // heads) + r // heads


def _mem_query_slabs(x, g, wq_ref, q_ref, heads, scale):
    db = x.shape[0]
    rr = q_ref.shape[0] // db
    hq = _rms(x, g).astype(BF16)
    for r in range(rr):
        j = _slab_col(r, heads, rr)
        q_ref[pl.ds(r, db, stride=rr), :] = _dot(hq, wq_ref[:, j * LANES:(j + 1) * LANES]) * scale


def _mem_attend_request(q_ref, a_ref, k, v, request, heads, n_mem):
    rr = k.shape[0] // n_mem
    row = pl.multiple_of(request * rr, rr)
    qv = q_ref[pl.ds(row, rr), :]
    part = jnp.sum(k.reshape(n_mem, rr, LANES) * qv[None], -1, keepdims=True)
    s = part
    for d in range(1, rr // heads):
        s = s + pltpu.roll(part, d * heads, 1)
    p = jnp.exp(s - jnp.max(s, 0, keepdims=True))
    l = jnp.sum(p, 0)
    a_ref[pl.ds(row, rr), :] = jnp.sum(p * v.reshape(n_mem, rr, LANES), 0) / l


def _mem_output(x, a_ref, wo_ref, heads):
    db = x.shape[0]
    rr = a_ref.shape[0] // db
    acc = x
    for r in range(rr):
        j = _slab_col(r, heads, rr)
        a = a_ref[pl.ds(r, db, stride=rr), :].astype(BF16)
        acc = acc + _dot(a, wo_ref[j * LANES:(j + 1) * LANES, :])
    return acc


def _ffn_mem_kernel(x_ref, g_ref, wg_ref, wu_ref, wd_ref, gf_ref,
                    xs_ref, gx_ref, wq_ref, k_ref, v_ref, wo_ref,
                    y_ref, os_ref, q_ref, a_ref, *, heads, scale, nb, n_mem):
    s_id = pl.program_id(0)

    @pl.when(s_id == 0)
    def _():
        _mem_query_slabs(xs_ref[...], gx_ref[...], wq_ref, q_ref, heads, scale)

    x = x_ref[...]
    h = _rms(x, g_ref[...]).astype(BF16)
    gate = _dot(h, wg_ref[...])
    a = (gate * jax.nn.sigmoid(gate) * _dot(h, wu_ref[...])).astype(BF16)
    y_ref[...] = _rms(x + _dot(a, wd_ref[...]), gf_ref[...])
    for bb in range(nb):
        _mem_attend_request(q_ref, a_ref, k_ref[bb], v_ref[bb], s_id * nb + bb, heads, n_mem)

    @pl.when(s_id == pl.num_programs(0) - 1)
    def _():
        os_ref[...] = _mem_output(xs_ref[...], a_ref, wo_ref, heads)


def _ffn_kernel(x_ref, g_ref, wg_ref, wu_ref, wd_ref, gf_ref, o_ref):
    x = x_ref[0]
    h = _rms(x, g_ref[...]).astype(BF16)
    gate = _dot(h, wg_ref[...])
    a = (gate * jax.nn.sigmoid(gate) * _dot(h, wu_ref[...])).astype(BF16)
    o_ref[0] = _rms(x + _dot(a, wd_ref[...]), gf_ref[...])


def _rope_tables(pos, rope, heads):
    half = rope // 2
    inv = jnp.power(ROPE_THETA, -(jnp.arange(half, dtype=F32) / half))
    ang = pos.astype(F32)[:, None] * inv[None, :]
    cos, sin = jnp.cos(ang), jnp.sin(ang)
    return (jnp.tile(jnp.concatenate([cos, cos], -1), (1, heads)),
            jnp.tile(jnp.concatenate([-sin, sin], -1), (1, heads)))


def _swap_halves(w):
    half = w.shape[-1] // 2
    return jnp.concatenate([w[..., half:], w[..., :half]], -1)


def _row(v):
    return v.reshape(1, -1)


def _tile(n, pref):
    return pref if n % pref == 0 else n


def _ffn(x, g, wg, wu, wd, gf, tm):
    bsz, t, d = x.shape
    spec = pl.BlockSpec((1, tm, d), lambda b, i: (b, i, 0))
    return pl.pallas_call(
        _ffn_kernel,
        grid=(bsz, t // tm),
        in_specs=[spec, _const_spec(g.shape), _const_spec(wg.shape), _const_spec(wu.shape),
                  _const_spec(wd.shape), _const_spec(gf.shape)],
        out_specs=spec,
        out_shape=jax.ShapeDtypeStruct(x.shape, F32),
        compiler_params=_params(2),
        name="ffn",
    )(x, g, wg, wu, wd, gf)


def _attention_schedule(bsz, n_tiles, n_req):
    steps = [(b, i, j) for b in range(bsz) for i in range(n_tiles) for j in range(i + 1)]
    n_steps = len(steps)
    assert n_req <= n_steps, "at most one sample request per attention step"
    by_cost = sorted(range(n_steps), key=lambda s: (steps[s][1] == steps[s][2], steps[s][1]), reverse=True)
    idle = set(by_cost[:n_steps - n_req])
    assert 0 not in idle
    tab = np.zeros((5, n_steps), np.int32)
    nxt = 0
    for s, (b, i, j) in enumerate(steps):
        hosted = s not in idle
        tab[:, s] = (b, i, j, nxt if hosted else -1, nxt if hosted else nxt - 1)
        nxt += hosted
    return tab


def _attention_with_paged(qlat_p, qrope_p, ctbf, krtbf, cbf, x, py, wuv, wout,
                          page_table, qlat_s, qrope_s, c_new, kr_new, cache_c, cache_rt, tq):
    bsz, heads, seq, kv = qlat_p.shape
    rope = qrope_p.shape[-1]
    d = x.shape[-1]
    db = qlat_s.shape[0]
    n_pages = page_table.shape[1]
    past = n_pages * PAGE_SIZE
    assert seq % tq == 0 and tq % ATTN_ROWS == 0
    n_chain = heads * (tq // ATTN_ROWS)
    tab = _attention_schedule(bsz, seq // tq, db)
    n_steps = tab.shape[1]
    col = lambda r: (lambda s, tab_ref, pt: tab_ref[r * n_steps + s])
    bb, ii, jj, rb = col(0), col(1), col(2), col(4)
    q_spec = lambda w: pl.BlockSpec((1, heads, tq, w), lambda *a: (bb(*a), 0, ii(*a), 0))
    kt_spec = lambda w: pl.BlockSpec((1, w, tq), lambda *a: (bb(*a), 0, jj(*a)))
    v_spec = pl.BlockSpec((1, tq, kv), lambda *a: (bb(*a), jj(*a), 0))
    row_spec = lambda w: pl.BlockSpec((1, tq, w), lambda *a: (bb(*a), ii(*a), 0))
    req_spec = lambda h, w: pl.BlockSpec((1, h, w), lambda *a: (rb(*a), 0, 0))
    return pl.pallas_call(
        functools.partial(_attn_prompt_kernel, tq=tq, heads=heads, n_pages=n_pages, n_req=db, n_steps=n_steps),
        grid_spec=pltpu.PrefetchScalarGridSpec(
            num_scalar_prefetch=2,
            grid=(n_steps,),
            in_specs=[q_spec(kv), q_spec(rope), kt_spec(kv), kt_spec(rope), v_spec, row_spec(d),
                      row_spec(py.shape[-1]), _const_spec(wuv.shape), _const_spec(wout.shape),
                      req_spec(heads, kv), req_spec(heads, rope), req_spec(1, kv), req_spec(1, rope),
                      pl.BlockSpec(memory_space=pl.ANY), pl.BlockSpec(memory_space=pl.ANY)],
            out_specs=[row_spec(d), req_spec(heads, kv)],
            scratch_shapes=[pltpu.VMEM((n_chain, ATTN_ROWS, LANES), F32),
                            pltpu.VMEM((n_chain, ATTN_ROWS, LANES), F32),
                            pltpu.VMEM((n_chain, ATTN_ROWS, kv), F32),
                            pltpu.VMEM((2, past, kv), F32), pltpu.VMEM((2, rope, past), F32),
                            pltpu.VMEM((past, kv), BF16), pltpu.SemaphoreType.DMA((2, 2))],
        ),
        out_shape=[jax.ShapeDtypeStruct((bsz, seq, d), F32), jax.ShapeDtypeStruct((db, heads, kv), F32)],
        compiler_params=_params(1),
        name="attn_prompt_paged",
    )(jnp.asarray(tab.reshape(-1)), page_table.reshape(-1),
      qlat_p, qrope_p, ctbf, krtbf, cbf, x, py, wuv, wout,
      qlat_s, qrope_s, c_new, kr_new, cache_c, cache_rt)


def _ffn_with_mem_attention(x, g, wg, wu, wd, gf, xs, gx, wxq, mem_k, mem_v, wxo, tm, heads, scale, n_mem):
    rows, d = x.shape
    db = xs.shape[0]
    n_steps = rows // tm
    assert rows % tm == 0 and db % n_steps == 0
    nb = db // n_steps
    slab = mem_k.shape[1] // n_mem
    x_spec = pl.BlockSpec((tm, d), lambda s: (s, 0))
    cache_spec = pl.BlockSpec((nb,) + mem_k.shape[1:], lambda s: (s, 0, 0))
    return pl.pallas_call(
        functools.partial(_ffn_mem_kernel, heads=heads, scale=scale, nb=nb, n_mem=n_mem),
        grid=(n_steps,),
        in_specs=[x_spec] + [_const_spec(a.shape) for a in (g, wg, wu, wd, gf, xs, gx, wxq)]
        + [cache_spec, cache_spec, _const_spec(wxo.shape)],
        out_specs=[x_spec, _whole_spec((db, d))],
        out_shape=[jax.ShapeDtypeStruct((rows, d), F32), jax.ShapeDtypeStruct((db, d), F32)],
        scratch_shapes=[pltpu.VMEM((db * slab, LANES), F32), pltpu.VMEM((db * slab, LANES), F32)],
        compiler_params=_params(1),
        name="ffn_mem",
    )(x, g, wg, wu, wd, gf, xs, gx, wxq, mem_k, mem_v, wxo)


def kernel(x_prompt, x_sample, mem_prompt, cache_kv_latent, cache_k_rope, cache_pool, cache_mem_k, cache_mem_v, page_table, norm_mix, w_in, q_norm, w_uq, kv_norm, w_uk, w_uv, w_pool, pool_scale, w_out, norm_x, mem_norm, w_xq, w_mk, w_mv, w_xo, norm_ffn, w_gate, w_up, w_down, norm_final):
    depth = w_in.shape[0]
    assert depth == 1, "single-layer trunk"
    bsz, seq, d = x_prompt.shape
    db, ts, _ = x_sample.shape
    assert ts == 1, "one new token per request"
    q_lora, kv_lora = q_norm.shape[1], kv_norm.shape[1]
    heads, nope = w_uk.shape[2], w_uk.shape[3]
    rope = cache_k_rope.shape[-1]
    pool_w = pool_scale.shape[1]
    v_head = w_uv.shape[3]
    n_mem, mem_heads, mem_hd = cache_mem_k.shape[2], cache_mem_k.shape[3], cache_mem_k.shape[4]
    n_pages = page_table.shape[1]
    assert cache_kv_latent.shape[2] == PAGE_SIZE
    past = n_pages * PAGE_SIZE
    mla_scale = float((nope + rope) ** -0.5)
    mem_scale = float(mem_hd ** -0.5)
    dims = (q_lora, kv_lora, pool_w, heads, nope, rope)
    l = 0

    i1, i2, i3 = q_lora, q_lora + kv_lora, q_lora + kv_lora + rope
    wi = w_in[l]
    w_kr = wi[:, i2:i3]
    win = jnp.concatenate([wi[:, :i2], wi[:, i3:], w_kr, _swap_halves(w_kr)], -1).astype(BF16)
    wq3 = w_uq[l].reshape(q_lora, heads, nope + rope)
    wq_rope = wq3[:, :, nope:]
    wuq = jnp.concatenate([wq3[:, :, :nope].reshape(q_lora, heads * nope),
                           wq_rope.reshape(q_lora, heads * rope),
                           _swap_halves(wq_rope).reshape(q_lora, heads * rope)], -1).astype(BF16)
    wuk = jnp.transpose(w_uk[l], (1, 2, 0)).astype(BF16)
    wuv = jnp.transpose(w_uv[l], (1, 0, 2)).astype(BF16)
    wpool = w_pool[l].astype(BF16)
    wout = w_out[l].astype(BF16)
    wxq, wxo = w_xq[l].astype(BF16), w_xo[l].astype(BF16)
    wmk, wmv = w_mk[l].astype(BF16), w_mv[l].astype(BF16)
    wg, wu, wd = w_gate[l].astype(BF16), w_up[l].astype(BF16), w_down[l].astype(BF16)
    gmix, gq, gkv = _row(norm_mix[l]), _row(q_norm[l]), _row(kv_norm[l])
    gx, gmem, gffn, gfin = _row(norm_x[l]), _row(mem_norm[l]), _row(norm_ffn[l]), _row(norm_final)
    pscale = _row(pool_scale[l])
    front_consts = (gmix, win, gq, wuq, gkv, wuk)
    hr = heads * rope

    tm = _tile(seq, 512)
    cos_p, sin_p = _rope_tables(jnp.arange(seq), rope, heads)
    row_spec = lambda w: pl.BlockSpec((1, tm, w), lambda b, t: (b, t, 0))
    head_spec = lambda w: pl.BlockSpec((1, heads, tm, w), lambda b, t: (b, 0, t, 0))
    tab_spec = pl.BlockSpec((tm, hr), lambda b, t: (t, 0))
    col_spec = lambda w: pl.BlockSpec((1, w, tm), lambda b, t: (b, 0, t))
    qlat_p, qrope_p, c_p, krt_p, cbf_p, ctbf_p, krtbf_p, py_p, pst_p = pl.pallas_call(
        functools.partial(_front_prompt_kernel, scale=mla_scale, dims=dims, tm=tm),
        grid=(bsz, seq // tm),
        in_specs=[row_spec(d)] + [_const_spec(a.shape) for a in front_consts]
        + [tab_spec, tab_spec, _const_spec(wpool.shape), _const_spec(pscale.shape)],
        out_specs=[head_spec(kv_lora), head_spec(rope), row_spec(kv_lora), col_spec(rope),
                   row_spec(kv_lora), col_spec(kv_lora), col_spec(rope), row_spec(pool_w),
                   pl.BlockSpec((1, POOL_HALO, pool_w), lambda b, t: (b, 0, 0))],
        out_shape=[jax.ShapeDtypeStruct((bsz, heads, seq, kv_lora), BF16),
                   jax.ShapeDtypeStruct((bsz, heads, seq, rope), BF16),
                   jax.ShapeDtypeStruct((bsz, seq, kv_lora), F32),
                   jax.ShapeDtypeStruct((bsz, rope, seq), F32),
                   jax.ShapeDtypeStruct((bsz, seq, kv_lora), BF16),
                   jax.ShapeDtypeStruct((bsz, kv_lora, seq), BF16),
                   jax.ShapeDtypeStruct((bsz, rope, seq), BF16),
                   jax.ShapeDtypeStruct((bsz, seq, pool_w), BF16),
                   jax.ShapeDtypeStruct((bsz, POOL_HALO, pool_w), F32)],
        scratch_shapes=[pltpu.VMEM((POOL_HALO, pool_w), F32)],
        compiler_params=_params(2),
        name="front_prompt",
    )(x_prompt, *front_consts, cos_p, sin_p, wpool, pscale)

    xs = x_sample.reshape(db, d)
    cos_s, sin_s = _rope_tables(jnp.full((db,), past), rope, heads)
    prev_s = jnp.transpose(cache_pool[l], (1, 0, 2))
    sample_in = (xs, *front_consts, cos_s, sin_s, wpool, pscale, prev_s)
    qlat_s, qrope_s, c_s, kr_s, py_s, up_s = pl.pallas_call(
        functools.partial(_front_sample_kernel, scale=mla_scale, dims=dims, past=past),
        grid=(1,),
        in_specs=[_const_spec(a.shape) for a in sample_in],
        out_specs=[_whole_spec(s) for s in ((heads, db, kv_lora), (heads, db, rope), (db, kv_lora),
                                            (db, rope), (db, pool_w), (db, pool_w))],
        out_shape=[jax.ShapeDtypeStruct((heads, db, kv_lora), BF16),
                   jax.ShapeDtypeStruct((heads, db, rope), BF16),
                   jax.ShapeDtypeStruct((db, kv_lora), F32),
                   jax.ShapeDtypeStruct((db, rope), F32),
                   jax.ShapeDtypeStruct((db, pool_w), BF16),
                   jax.ShapeDtypeStruct((db, pool_w), F32)],
        compiler_params=_params(1),
        name="front_sample",
    )(*sample_in)

    x1_p, olat_s = _attention_with_paged(
        qlat_p, qrope_p, ctbf_p, krtbf_p, cbf_p, x_prompt, py_p, wuv, wout,
        page_table, jnp.swapaxes(qlat_s, 0, 1), jnp.swapaxes(qrope_s, 0, 1),
        c_s.reshape(db, 1, kv_lora), kr_s.reshape(db, 1, rope),
        cache_kv_latent[l], jnp.swapaxes(cache_k_rope[l], 1, 2),
        tq=_tile(seq, 512))

    mem_spec = pl.BlockSpec((1, n_mem, d), lambda b: (b, 0, 0))
    mk_p, mv_p, mkbf_p, mvbf_p = pl.pallas_call(
        _mem_kv_kernel,
        grid=(bsz,),
        in_specs=[mem_spec, _const_spec(gmem.shape), _const_spec(wmk.shape), _const_spec(wmv.shape)],
        out_specs=[mem_spec] * 4,
        out_shape=[jax.ShapeDtypeStruct((bsz, n_mem, d), F32)] * 2
        + [jax.ShapeDtypeStruct((bsz, n_mem, d), BF16)] * 2,
        compiler_params=_params(1),
        name="mem_kv",
    )(mem_prompt, gmem, wmk, wmv)

    tmem = _tile(seq, 512)
    xm_spec = pl.BlockSpec((1, tmem, d), lambda b, i: (b, i, 0))
    memkv_spec = pl.BlockSpec((1, n_mem, d), lambda b, i: (b, 0, 0))
    x2_p = pl.pallas_call(
        functools.partial(_mem_attn_prompt_kernel, heads=mem_heads, scale=mem_scale),
        grid=(bsz, seq // tmem),
        in_specs=[xm_spec, _const_spec(gx.shape), _const_spec(wxq.shape), memkv_spec, memkv_spec,
                  _const_spec(wxo.shape)],
        out_specs=xm_spec,
        out_shape=jax.ShapeDtypeStruct((bsz, seq, d), F32),
        compiler_params=_params(2),
        name="mem_attn_prompt",
    )(x1_p, gx, wxq, mkbf_p, mvbf_p, wxo)

    mix_in = (jnp.swapaxes(olat_s, 0, 1), py_s, xs, wuv, wout)
    x1_s = pl.pallas_call(
        functools.partial(_mix_out_kernel, heads=heads),
        grid=(1,),
        in_specs=[_const_spec(a.shape) for a in mix_in],
        out_specs=_whole_spec((db, d)),
        out_shape=jax.ShapeDtypeStruct((db, d), F32),
        compiler_params=_params(1),
        name="mix_out_sample",
    )(*mix_in)

    assert mem_hd % LANES == 0
    slab = mem_heads * (mem_hd // LANES)

    def slab_view(a):
        a = a.reshape(db, n_mem, mem_heads, mem_hd // LANES, LANES)
        return jnp.transpose(a, (0, 1, 3, 2, 4)).reshape(db, n_mem * slab, LANES)

    y_prompt, x2_s = _ffn_with_mem_attention(
        x2_p.reshape(bsz * seq, d), gffn, wg, wu, wd, gfin,
        x1_s, gx, wxq, slab_view(cache_mem_k[l]), slab_view(cache_mem_v[l]), wxo,
        tm=_tile(bsz * seq, 256), heads=mem_heads, scale=mem_scale, n_mem=n_mem)
    y_prompt = y_prompt.reshape(bsz, seq, d)

    y_sample = _ffn(x2_s.reshape(1, db, d), gffn, wg, wu, wd, gfin, db).reshape(db, ts, d)

    mem_shape = (depth, bsz, n_mem, mem_heads, mem_hd)
    return (y_prompt, y_sample,
            c_p[None], jnp.swapaxes(krt_p, 1, 2)[None], pst_p[None, :, POOL_HALO - POOL_STATE:],
            mk_p.reshape(mem_shape), mv_p.reshape(mem_shape),
            c_s.reshape(depth, db, ts, kv_lora), kr_s.reshape(depth, db, ts, rope),
            jnp.concatenate([cache_pool[l][:, 1:], up_s[:, None]], 1)[None])
```

```python
import functools

import jax
import jax.numpy as jnp
import numpy as np
from jax import lax
from jax.experimental import pallas as pl
from jax.experimental.pallas import tpu as pltpu

F32 = jnp.float32
BF16 = jnp.bfloat16

EPS = 1e-6
ROPE_THETA = 10000.0
PAGE_SIZE = 128
POOL_WINDOWS = (2, 4, 8, 16)
POOL_STATE = max(POOL_WINDOWS) - 1
POOL_HALO = 16
LANES = 128
VMEM_LIMIT = 56 * 1024 * 1024
ATTN_ROWS = 256
ATTN_AHEAD = 3


def _rms(x, g):
    return x * lax.rsqrt(jnp.mean(x * x, -1, keepdims=True) + EPS) * g


def _dot(a, b):
    return jnp.dot(a, b, preferred_element_type=F32)


def _dot_nt(a, b):
    return lax.dot_general(a, b, (((1,), (1,)), ((), ())), preferred_element_type=F32)


def _lane_tile(x, n):
    return jnp.concatenate([x] * n, -1)


def _const_spec(shape):
    nd = len(shape)
    return pl.BlockSpec(shape, lambda *_: (0,) * nd, pipeline_mode=pl.Buffered(1))


def _whole_spec(shape):
    nd = len(shape)
    return pl.BlockSpec(shape, lambda *_: (0,) * nd)


def _params(n_axes):
    return pltpu.CompilerParams(dimension_semantics=("arbitrary",) * n_axes,
                                vmem_limit_bytes=VMEM_LIMIT)


def _mla_inputs(x, gmix, win, gq, wuq, gkv, wuk, cos, sin, scale, dims):
    q_lora, kv_lora, pool_w, heads, nope, rope = dims
    h = _rms(x, gmix).astype(BF16)
    u = _dot(h, win)
    i1, i2, i3 = q_lora, q_lora + kv_lora, q_lora + kv_lora + pool_w
    cq, ckv, up, krs = u[:, :i1], u[:, i1:i2], u[:, i2:i3], u[:, i3:]
    q = _dot(_rms(cq, gq).astype(BF16), wuq)
    n0 = heads * nope
    n1 = n0 + heads * rope
    q_rope = (q[:, n0:n1] * cos + q[:, n1:] * sin) * scale
    q_lat = [(_dot(q[:, hh * nope:(hh + 1) * nope].astype(BF16), wuk[hh]) * scale).astype(BF16)
             for hh in range(heads)]
    q_rope = [q_rope[:, hh * rope:(hh + 1) * rope].astype(BF16) for hh in range(heads)]
    c = _rms(ckv, gkv)
    k_rope = krs * cos[:, :2 * rope] + pltpu.roll(krs, rope, 1) * sin[:, :2 * rope]
    return q_lat, q_rope, c, k_rope, up


def _pool_project(sums, up, cnt, wpool, pscale):
    cg = wpool.shape[-1]
    ys = []
    for g in range(len(POOL_WINDOWS)):
        m = sums[g] / cnt[g] - up[:, g * cg:(g + 1) * cg]
        ys.append(_dot(m.astype(BF16), wpool[g]))
    return jnp.concatenate(ys, -1) * pscale


def _front_prompt_kernel(x_ref, gmix_ref, win_ref, gq_ref, wuq_ref, gkv_ref, wuk_ref, cos_ref, sin_ref,
                         wpool_ref, pscale_ref,
                         qlat_ref, qrope_ref, c_ref, krt_ref, cbf_ref, ctbf_ref, krtbf_ref, py_ref, pst_ref,
                         prev_ref, *, scale, dims, tm):
    t = pl.program_id(1)
    heads, rope = dims[3], dims[5]
    q_lat, q_rope, c, k_rope, up = _mla_inputs(
        x_ref[0], gmix_ref[...], win_ref[...], gq_ref[...], wuq_ref[...], gkv_ref[...], wuk_ref,
        cos_ref[...], sin_ref[...], scale, dims)
    for hh in range(heads):
        qlat_ref[0, hh] = q_lat[hh]
        qrope_ref[0, hh] = q_rope[hh]
    c_ref[0] = c
    cbf_ref[0] = c.astype(BF16)
    ctbf_ref[0] = c.T.astype(BF16)
    krt = k_rope.T[:rope]
    krt_ref[0] = krt
    krtbf_ref[0] = krt.astype(BF16)

    @pl.when(t == 0)
    def _():
        prev_ref[...] = jnp.zeros_like(prev_ref)

    cg = wpool_ref.shape[-1]
    e = jnp.concatenate([prev_ref[...], up], 0)
    tail = up[tm - POOL_HALO:, :]
    prev_ref[...] = tail
    pst_ref[0] = tail
    sums = []
    s = e
    for g, w in enumerate(POOL_WINDOWS):
        s = s[:, (cg if g else 0):]
        s = s + pltpu.roll(s, w // 2, 0)
        sums.append(s[POOL_HALO:, :cg])
    pos1 = t * tm + lax.broadcasted_iota(jnp.int32, (tm, cg), 0) + 1
    cnt = [jnp.minimum(w, pos1).astype(F32) for w in POOL_WINDOWS]
    py_ref[0] = _pool_project(sums, up, cnt, wpool_ref, pscale_ref[...]).astype(BF16)


def _front_sample_kernel(x_ref, gmix_ref, win_ref, gq_ref, wuq_ref, gkv_ref, wuk_ref, cos_ref, sin_ref,
                         wpool_ref, pscale_ref, prev_ref,
                         qlat_ref, qrope_ref, c_ref, kr_ref, py_ref, up_ref, *, scale, dims, past):
    heads, rope = dims[3], dims[5]
    q_lat, q_rope, c, k_rope, up = _mla_inputs(
        x_ref[...], gmix_ref[...], win_ref[...], gq_ref[...], wuq_ref[...], gkv_ref[...], wuk_ref,
        cos_ref[...], sin_ref[...], scale, dims)
    for hh in range(heads):
        qlat_ref[hh] = q_lat[hh]
        qrope_ref[hh] = q_rope[hh]
    c_ref[...] = c
    kr_ref[...] = k_rope[:, :rope]
    up_ref[...] = up

    cg = wpool_ref.shape[-1]
    sums, cnt = [], []
    for g, w in enumerate(POOL_WINDOWS):
        s = up[:, g * cg:(g + 1) * cg]
        for j in range(1, w):
            s = s + prev_ref[POOL_STATE - j][:, g * cg:(g + 1) * cg]
        sums.append(s)
        cnt.append(float(min(w, past + 1)))
    py_ref[...] = _pool_project(sums, up, cnt, wpool_ref, pscale_ref[...]).astype(BF16)


def _mix_out(o_lat_heads, py, x, wuv_ref, wout_ref):
    o_mla = jnp.concatenate(
        [_dot(o.astype(BF16), wuv_ref[hh]) for hh, o in enumerate(o_lat_heads)], -1).astype(BF16)
    wm = o_mla.shape[-1]
    return x + _dot(o_mla, wout_ref[:wm, :]) + _dot(py, wout_ref[wm:, :])


def _paged_attention(ql, qr, cn, krn, cbf, rot, pieces):
    cnr = cn.astype(BF16).astype(F32)
    krnr = krn.astype(BF16).astype(F32)
    w = cbf.shape[0] // pieces

    def scores(k):
        return (_dot_nt(ql, cbf[k * w:(k + 1) * w, :])
                + _dot(qr, rot[:, k * w:(k + 1) * w].astype(BF16)))

    def softmax(s_parts):
        s = jnp.concatenate(s_parts, -1)
        s_new = (jnp.sum(ql.astype(F32) * cnr, -1, keepdims=True)
                 + jnp.sum(qr.astype(F32) * krnr, -1, keepdims=True))
        m = jnp.maximum(jnp.max(s, -1, keepdims=True), s_new)
        p = jnp.exp(s - m)
        p_new = jnp.exp(s_new - m)
        return p.astype(BF16), jnp.sum(p, -1, keepdims=True) + p_new, p_new

    def values(p, k):
        return _dot(p[:, k * w:(k + 1) * w], cbf[k * w:(k + 1) * w, :])

    def finish(v_parts, l, p_new):
        return (sum(v_parts) + p_new.astype(BF16).astype(F32) * cnr) / l

    return scores, softmax, values, finish


def _attn_prompt_kernel(tab_ref, pt_ref,
                        ql_ref, qr_ref, kt_ref, rt_ref, v_ref, x_ref, py_ref, wuv_ref, wout_ref,
                        sql_ref, sqr_ref, cn_ref, krn_ref, cache_c, cache_rt,
                        o_ref, so_ref,
                        m_ref, l_ref, acc_ref, kbuf, rbuf, cbf, sem,
                        *, tq, heads, n_pages, n_req, n_steps):
    step = pl.program_id(0)
    i = tab_ref[n_steps + step]
    j = tab_ref[2 * n_steps + step]
    req = tab_ref[3 * n_steps + step]
    tk = tq
    chains = [(hh, r) for hh in range(heads) for r in range(tq // ATTN_ROWS)]

    def page_copies(rq, sl):
        lat, rot = [], []
        for p in range(n_pages):
            page = pt_ref[rq * n_pages + p]
            toks = pl.ds(p * PAGE_SIZE, PAGE_SIZE)
            lat.append(pltpu.make_async_copy(cache_c.at[page], kbuf.at[sl, toks, :], sem.at[0, sl]))
            rot.append(pltpu.make_async_copy(cache_rt.at[page], rbuf.at[sl, :, toks], sem.at[1, sl]))
        return lat + rot

    @pl.when(step == 0)
    def _():
        for cp in page_copies(0, 0):
            cp.start()

    @pl.when(j == 0)
    def _():
        m_ref[...] = jnp.full_like(m_ref, -jnp.inf)
        l_ref[...] = jnp.zeros_like(l_ref)
        acc_ref[...] = jnp.zeros_like(acc_ref)

    def attend(diagonal, with_sample):
        if with_sample:
            sl = req % 2
            for cp in page_copies(req, sl):
                cp.wait()
            nxt = jnp.minimum(req + 1, n_req - 1)
            for cp in page_copies(nxt, 1 - sl):
                cp.start()

            cbf[...] = kbuf[sl].astype(BF16)
            pieces = len(chains) // 2
            s_scores, s_softmax, s_values, s_finish = _paged_attention(
                sql_ref[req], sqr_ref[req], cn_ref[req], krn_ref[req], cbf, rbuf.at[sl], pieces)
            s_parts, v_parts = [], []

        off = pl.multiple_of(j * tk, tk)
        kt = kt_ref[0, :, pl.ds(off, tk)]
        rt = rt_ref[0, :, pl.ds(off, tk)]
        v = v_ref[0, pl.ds(off, tk), :]

        def width(r):
            return (r + 1) * ATTN_ROWS if diagonal else tk

        def scores(ci):
            hh, r = chains[ci]
            rows = pl.ds(r * ATTN_ROWS, ATTN_ROWS)
            w = width(r)
            return _dot(ql_ref[0, hh, rows, :], kt[:, :w]) + _dot(qr_ref[0, hh, rows, :], rt[:, :w])

        pending = [scores(ci) for ci in range(ATTN_AHEAD)]
        for ci, (hh, r) in enumerate(chains):
            if ci + ATTN_AHEAD < len(chains):
                pending.append(scores(ci + ATTN_AHEAD))
            if with_sample and ci < pieces:
                s_parts.append(s_scores(ci))
            if with_sample and ci == pieces:
                p_sample, l_sample, p_new = s_softmax(s_parts)
            if with_sample and ci >= pieces:
                v_parts.append(s_values(p_sample, ci - pieces))
            s = pending[ci]
            if diagonal:
                qpos = r * ATTN_ROWS + lax.broadcasted_iota(jnp.int32, s.shape, 0)
                kpos = lax.broadcasted_iota(jnp.int32, s.shape, 1)
                s = jnp.where(kpos <= qpos, s, -jnp.inf)
            m_prev = m_ref[ci]
            m_new = jnp.maximum(m_prev, jnp.max(s, -1, keepdims=True))
            alpha = jnp.exp(m_prev - m_new)
            p = jnp.exp(s - _lane_tile(m_new, width(r) // LANES))
            l_ref[ci] = alpha * l_ref[ci] + jnp.sum(p, -1, keepdims=True)
            acc_ref[ci] = (_lane_tile(alpha, acc_ref.shape[-1] // LANES) * acc_ref[ci]
                           + _dot(p.astype(BF16), v[:width(r)]))
            m_ref[ci] = m_new
        if with_sample:
            so_ref[req] = s_finish(v_parts, l_sample, p_new)

            @pl.when(req == n_req - 1)
            def _():
                for cp in page_copies(nxt, 1 - sl):
                    cp.wait()

    for diagonal in (False, True):
        for with_sample in (False, True):
            on_diag = (j == i) if diagonal else (j < i)
            hosting = (req >= 0) if with_sample else (req < 0)
            pl.when(on_diag & hosting)(functools.partial(attend, diagonal, with_sample))

    @pl.when(j == i)
    def _():
        nr = tq // ATTN_ROWS
        rep = acc_ref.shape[-1] // LANES
        o_heads = [jnp.concatenate([acc_ref[hh * nr + r] / _lane_tile(l_ref[hh * nr + r], rep)
                                    for r in range(nr)], 0)
                   for hh in range(heads)]
        o_ref[0] = _mix_out(o_heads, py_ref[0], x_ref[0], wuv_ref, wout_ref)


def _mix_out_kernel(ol_ref, py_ref, x_ref, wuv_ref, wout_ref, o_ref, *, heads):
    o_ref[...] = _mix_out([ol_ref[hh] for hh in range(heads)], py_ref[...], x_ref[...],
                          wuv_ref, wout_ref)


def _mem_kv_kernel(mem_ref, g_ref, wk_ref, wv_ref, k_ref, v_ref, kbf_ref, vbf_ref):
    m = _rms(mem_ref[0], g_ref[...]).astype(BF16)
    k = _dot(m, wk_ref[...])
    v = _dot(m, wv_ref[...])
    k_ref[0] = k
    v_ref[0] = v
    kbf_ref[0] = k.astype(BF16)
    vbf_ref[0] = v.astype(BF16)


def _mem_attn_prompt_kernel(x_ref, g_ref, wq_ref, k_ref, v_ref, wo_ref, o_ref, *, heads, scale):
    x = x_ref[0]
    q = (_dot(_rms(x, g_ref[...]).astype(BF16), wq_ref[...]) * scale).astype(BF16)
    hd = q.shape[-1] // heads
    outs = []
    for hh in range(heads):
        sl = slice(hh * hd, (hh + 1) * hd)
        s = _dot_nt(q[:, sl], k_ref[0, :, sl])
        p = jnp.exp(s - jnp.max(s, -1, keepdims=True))
        o = _dot(p.astype(BF16), v_ref[0, :, sl]) / jnp.sum(p, -1, keepdims=True)
        outs.append(o.astype(BF16))
    o_ref[0] = x + _dot(jnp.concatenate(outs, -1), wo_ref[...])


def _slab_col(r, heads, rr):
    return (r % heads) * (rr // heads) + r // heads


def _mem_query_slabs(x, g, wq_ref, q_ref, heads, scale):
    db = x.shape[0]
    rr = q_ref.shape[0] // db
    hq = _rms(x, g).astype(BF16)
    for r in range(rr):
        j = _slab_col(r, heads, rr)
        q_ref[pl.ds(r, db, stride=rr), :] = _dot(hq, wq_ref[:, j * LANES:(j + 1) * LANES]) * scale


def _mem_attend_request(q_ref, a_ref, k, v, request, heads, n_mem):
    rr = k.shape[0] // n_mem
    row = pl.multiple_of(request * rr, rr)
    qv = q_ref[pl.ds(row, rr), :]
    part = jnp.sum(k.reshape(n_mem, rr, LANES) * qv[None], -1, keepdims=True)
    s = part
    for d in range(1, rr // heads):
        s = s + pltpu.roll(part, d * heads, 1)
    p = jnp.exp(s - jnp.max(s, 0, keepdims=True))
    l = jnp.sum(p, 0)
    a_ref[pl.ds(row, rr), :] = jnp.sum(p * v.reshape(n_mem, rr, LANES), 0) / l


def _mem_output(x, a_ref, wo_ref, heads):
    db = x.shape[0]
    rr = a_ref.shape[0] // db
    acc = x
    for r in range(rr):
        j = _slab_col(r, heads, rr)
        a = a_ref[pl.ds(r, db, stride=rr), :].astype(BF16)
        acc = acc + _dot(a, wo_ref[j * LANES:(j + 1) * LANES, :])
    return acc


def _ffn_mem_kernel(x_ref, g_ref, wg_ref, wu_ref, wd_ref, gf_ref,
                    xs_ref, gx_ref, wq_ref, k_ref, v_ref, wo_ref,
                    y_ref, os_ref, q_ref, a_ref, *, heads, scale, nb, n_mem):
    s_id = pl.program_id(0)

    @pl.when(s_id == 0)
    def _():
        _mem_query_slabs(xs_ref[...], gx_ref[...], wq_ref, q_ref, heads, scale)

    x = x_ref[...]
    h = _rms(x, g_ref[...]).astype(BF16)
    gate = _dot(h, wg_ref[...])
    a = (gate * jax.nn.sigmoid(gate) * _dot(h, wu_ref[...])).astype(BF16)
    y_ref[...] = _rms(x + _dot(a, wd_ref[...]), gf_ref[...])
    for bb in range(nb):
        _mem_attend_request(q_ref, a_ref, k_ref[bb], v_ref[bb], s_id * nb + bb, heads, n_mem)

    @pl.when(s_id == pl.num_programs(0) - 1)
    def _():
        os_ref[...] = _mem_output(xs_ref[...], a_ref, wo_ref, heads)


def _ffn_kernel(x_ref, g_ref, wg_ref, wu_ref, wd_ref, gf_ref, o_ref):
    x = x_ref[0]
    h = _rms(x, g_ref[...]).astype(BF16)
    gate = _dot(h, wg_ref[...])
    a = (gate * jax.nn.sigmoid(gate) * _dot(h, wu_ref[...])).astype(BF16)
    o_ref[0] = _rms(x + _dot(a, wd_ref[...]), gf_ref[...])


def _rope_tables(pos, rope, heads):
    half = rope // 2
    inv = jnp.power(ROPE_THETA, -(jnp.arange(half, dtype=F32) / half))
    ang = pos.astype(F32)[:, None] * inv[None, :]
    cos, sin = jnp.cos(ang), jnp.sin(ang)
    return (jnp.tile(jnp.concatenate([cos, cos], -1), (1, heads)),
            jnp.tile(jnp.concatenate([-sin, sin], -1), (1, heads)))


def _swap_halves(w):
    half = w.shape[-1] // 2
    return jnp.concatenate([w[..., half:], w[..., :half]], -1)


def _row(v):
    return v.reshape(1, -1)


def _tile(n, pref):
    return pref if n % pref == 0 else n


def _ffn(x, g, wg, wu, wd, gf, tm):
    bsz, t, d = x.shape
    spec = pl.BlockSpec((1, tm, d), lambda b, i: (b, i, 0))
    return pl.pallas_call(
        _ffn_kernel,
        grid=(bsz, t // tm),
        in_specs=[spec, _const_spec(g.shape), _const_spec(wg.shape), _const_spec(wu.shape),
                  _const_spec(wd.shape), _const_spec(gf.shape)],
        out_specs=spec,
        out_shape=jax.ShapeDtypeStruct(x.shape, F32),
        compiler_params=_params(2),
        name="ffn",
    )(x, g, wg, wu, wd, gf)


def _attention_schedule(bsz, n_tiles, n_req):
    steps = [(b, i, j) for b in range(bsz) for i in range(n_tiles) for j in range(i + 1)]
    n_steps = len(steps)
    assert n_req <= n_steps, "at most one sample request per attention step"
    by_cost = sorted(range(n_steps), key=lambda s: (steps[s][1] == steps[s][2], steps[s][1]), reverse=True)
    idle = set(by_cost[:n_steps - n_req])
    assert 0 not in idle, "the first step starts the page stream"
    tab = np.zeros((4, n_steps), np.int32)
    nxt = 0
    for s, (b, i, j) in enumerate(steps):
        hosted = s not in idle
        tab[:, s] = (b, i, j, nxt if hosted else -1)
        nxt += hosted
    return tab


def _attention_with_paged(qlat_p, qrope_p, ctbf, krtbf, cbf, x, py, wuv, wout,
                          page_table, qlat_s, qrope_s, c_new, kr_new, cache_c, cache_rt, tq):
    bsz, heads, seq, kv = qlat_p.shape
    rope = qrope_p.shape[-1]
    d = x.shape[-1]
    db = qlat_s.shape[0]
    n_pages = page_table.shape[1]
    past = n_pages * PAGE_SIZE
    assert seq % tq == 0 and tq % ATTN_ROWS == 0
    n_chain = heads * (tq // ATTN_ROWS)
    tab = _attention_schedule(bsz, seq // tq, db)
    n_steps = tab.shape[1]
    col = lambda r: (lambda s, tab_ref, pt: tab_ref[r * n_steps + s])
    bb, ii = col(0), col(1)
    q_spec = lambda w: pl.BlockSpec((1, heads, tq, w), lambda *a: (bb(*a), 0, ii(*a), 0))
    batch_spec = lambda r, c: pl.BlockSpec((1, r, c), lambda *a: (bb(*a), 0, 0), pipeline_mode=pl.Buffered(1))
    row_spec = lambda w: pl.BlockSpec((1, tq, w), lambda *a: (bb(*a), ii(*a), 0))
    return pl.pallas_call(
        functools.partial(_attn_prompt_kernel, tq=tq, heads=heads, n_pages=n_pages, n_req=db, n_steps=n_steps),
        grid_spec=pltpu.PrefetchScalarGridSpec(
            num_scalar_prefetch=2,
            grid=(n_steps,),
            in_specs=[q_spec(kv), q_spec(rope), batch_spec(kv, seq), batch_spec(rope, seq), batch_spec(seq, kv),
                      row_spec(d), row_spec(py.shape[-1]), _const_spec(wuv.shape), _const_spec(wout.shape),
                      _const_spec(qlat_s.shape), _const_spec(qrope_s.shape), _const_spec(c_new.shape),
                      _const_spec(kr_new.shape),
                      pl.BlockSpec(memory_space=pl.ANY), pl.BlockSpec(memory_space=pl.ANY)],
            out_specs=[row_spec(d), _whole_spec((db, heads, kv))],
            scratch_shapes=[pltpu.VMEM((n_chain, ATTN_ROWS, LANES), F32),
                            pltpu.VMEM((n_chain, ATTN_ROWS, LANES), F32),
                            pltpu.VMEM((n_chain, ATTN_ROWS, kv), F32),
                            pltpu.VMEM((2, past, kv), F32), pltpu.VMEM((2, rope, past), F32),
                            pltpu.VMEM((past, kv), BF16), pltpu.SemaphoreType.DMA((2, 2))],
        ),
        out_shape=[jax.ShapeDtypeStruct((bsz, seq, d), F32), jax.ShapeDtypeStruct((db, heads, kv), F32)],
        compiler_params=_params(1),
        name="attn_prompt_paged",
    )(jnp.asarray(tab.reshape(-1)), page_table.reshape(-1),
      qlat_p, qrope_p, ctbf, krtbf, cbf, x, py, wuv, wout,
      qlat_s, qrope_s, c_new, kr_new, cache_c, cache_rt)


def _ffn_with_mem_attention(x, g, wg, wu, wd, gf, xs, gx, wxq, mem_k, mem_v, wxo, tm, heads, scale, n_mem):
    rows, d = x.shape
    db = xs.shape[0]
    n_steps = rows // tm
    assert rows % tm == 0 and db % n_steps == 0
    nb = db // n_steps
    slab = mem_k.shape[1] // n_mem
    x_spec = pl.BlockSpec((tm, d), lambda s: (s, 0))
    cache_spec = pl.BlockSpec((nb,) + mem_k.shape[1:], lambda s: (s, 0, 0))
    return pl.pallas_call(
        functools.partial(_ffn_mem_kernel, heads=heads, scale=scale, nb=nb, n_mem=n_mem),
        grid=(n_steps,),
        in_specs=[x_spec] + [_const_spec(a.shape) for a in (g, wg, wu, wd, gf, xs, gx, wxq)]
        + [cache_spec, cache_spec, _const_spec(wxo.shape)],
        out_specs=[x_spec, _whole_spec((db, d))],
        out_shape=[jax.ShapeDtypeStruct((rows, d), F32), jax.ShapeDtypeStruct((db, d), F32)],
        scratch_shapes=[pltpu.VMEM((db * slab, LANES), F32), pltpu.VMEM((db * slab, LANES), F32)],
        compiler_params=_params(1),
        name="ffn_mem",
    )(x, g, wg, wu, wd, gf, xs, gx, wxq, mem_k, mem_v, wxo)


def kernel(x_prompt, x_sample, mem_prompt, cache_kv_latent, cache_k_rope, cache_pool, cache_mem_k, cache_mem_v, page_table, norm_mix, w_in, q_norm, w_uq, kv_norm, w_uk, w_uv, w_pool, pool_scale, w_out, norm_x, mem_norm, w_xq, w_mk, w_mv, w_xo, norm_ffn, w_gate, w_up, w_down, norm_final):
    depth = w_in.shape[0]
    assert depth == 1, "single-layer trunk"
    bsz, seq, d = x_prompt.shape
    db, ts, _ = x_sample.shape
    assert ts == 1, "one new token per request"
    q_lora, kv_lora = q_norm.shape[1], kv_norm.shape[1]
    heads, nope = w_uk.shape[2], w_uk.shape[3]
    rope = cache_k_rope.shape[-1]
    pool_w = pool_scale.shape[1]
    v_head = w_uv.shape[3]
    n_mem, mem_heads, mem_hd = cache_mem_k.shape[2], cache_mem_k.shape[3], cache_mem_k.shape[4]
    n_pages = page_table.shape[1]
    assert cache_kv_latent.shape[2] == PAGE_SIZE
    past = n_pages * PAGE_SIZE
    mla_scale = float((nope + rope) ** -0.5)
    mem_scale = float(mem_hd ** -0.5)
    dims = (q_lora, kv_lora, pool_w, heads, nope, rope)
    l = 0

    i1, i2, i3 = q_lora, q_lora + kv_lora, q_lora + kv_lora + rope
    wi = w_in[l]
    w_kr = wi[:, i2:i3]
    win = jnp.concatenate([wi[:, :i2], wi[:, i3:], w_kr, _swap_halves(w_kr)], -1).astype(BF16)
    wq3 = w_uq[l].reshape(q_lora, heads, nope + rope)
    wq_rope = wq3[:, :, nope:]
    wuq = jnp.concatenate([wq3[:, :, :nope].reshape(q_lora, heads * nope),
                           wq_rope.reshape(q_lora, heads * rope),
                           _swap_halves(wq_rope).reshape(q_lora, heads * rope)], -1).astype(BF16)
    wuk = jnp.transpose(w_uk[l], (1, 2, 0)).astype(BF16)
    wuv = jnp.transpose(w_uv[l], (1, 0, 2)).astype(BF16)
    wpool = w_pool[l].astype(BF16)
    wout = w_out[l].astype(BF16)
    wxq, wxo = w_xq[l].astype(BF16), w_xo[l].astype(BF16)
    wmk, wmv = w_mk[l].astype(BF16), w_mv[l].astype(BF16)
    wg, wu, wd = w_gate[l].astype(BF16), w_up[l].astype(BF16), w_down[l].astype(BF16)
    gmix, gq, gkv = _row(norm_mix[l]), _row(q_norm[l]), _row(kv_norm[l])
    gx, gmem, gffn, gfin = _row(norm_x[l]), _row(mem_norm[l]), _row(norm_ffn[l]), _row(norm_final)
    pscale = _row(pool_scale[l])
    front_consts = (gmix, win, gq, wuq, gkv, wuk)
    hr = heads * rope

    tm = _tile(seq, 512)
    cos_p, sin_p = _rope_tables(jnp.arange(seq), rope, heads)
    row_spec = lambda w: pl.BlockSpec((1, tm, w), lambda b, t: (b, t, 0))
    head_spec = lambda w: pl.BlockSpec((1, heads, tm, w), lambda b, t: (b, 0, t, 0))
    tab_spec = pl.BlockSpec((tm, hr), lambda b, t: (t, 0))
    col_spec = lambda w: pl.BlockSpec((1, w, tm), lambda b, t: (b, 0, t))
    qlat_p, qrope_p, c_p, krt_p, cbf_p, ctbf_p, krtbf_p, py_p, pst_p = pl.pallas_call(
        functools.partial(_front_prompt_kernel, scale=mla_scale, dims=dims, tm=tm),
        grid=(bsz, seq // tm),
        in_specs=[row_spec(d)] + [_const_spec(a.shape) for a in front_consts]
        + [tab_spec, tab_spec, _const_spec(wpool.shape), _const_spec(pscale.shape)],
        out_specs=[head_spec(kv_lora), head_spec(rope), row_spec(kv_lora), col_spec(rope),
                   row_spec(kv_lora), col_spec(kv_lora), col_spec(rope), row_spec(pool_w),
                   pl.BlockSpec((1, POOL_HALO, pool_w), lambda b, t: (b, 0, 0))],
        out_shape=[jax.ShapeDtypeStruct((bsz, heads, seq, kv_lora), BF16),
                   jax.ShapeDtypeStruct((bsz, heads, seq, rope), BF16),
                   jax.ShapeDtypeStruct((bsz, seq, kv_lora), F32),
                   jax.ShapeDtypeStruct((bsz, rope, seq), F32),
                   jax.ShapeDtypeStruct((bsz, seq, kv_lora), BF16),
                   jax.ShapeDtypeStruct((bsz, kv_lora, seq), BF16),
                   jax.ShapeDtypeStruct((bsz, rope, seq), BF16),
                   jax.ShapeDtypeStruct((bsz, seq, pool_w), BF16),
                   jax.ShapeDtypeStruct((bsz, POOL_HALO, pool_w), F32)],
        scratch_shapes=[pltpu.VMEM((POOL_HALO, pool_w), F32)],
        compiler_params=_params(2),
        name="front_prompt",
    )(x_prompt, *front_consts, cos_p, sin_p, wpool, pscale)

    xs = x_sample.reshape(db, d)
    cos_s, sin_s = _rope_tables(jnp.full((db,), past), rope, heads)
    prev_s = jnp.transpose(cache_pool[l], (1, 0, 2))
    sample_in = (xs, *front_consts, cos_s, sin_s, wpool, pscale, prev_s)
    qlat_s, qrope_s, c_s, kr_s, py_s, up_s = pl.pallas_call(
        functools.partial(_front_sample_kernel, scale=mla_scale, dims=dims, past=past),
        grid=(1,),
        in_specs=[_const_spec(a.shape) for a in sample_in],
        out_specs=[_whole_spec(s) for s in ((heads, db, kv_lora), (heads, db, rope), (db, kv_lora),
                                            (db, rope), (db, pool_w), (db, pool_w))],
        out_shape=[jax.ShapeDtypeStruct((heads, db, kv_lora), BF16),
                   jax.ShapeDtypeStruct((heads, db, rope), BF16),
                   jax.ShapeDtypeStruct((db, kv_lora), F32),
                   jax.ShapeDtypeStruct((db, rope), F32),
                   jax.ShapeDtypeStruct((db, pool_w), BF16),
                   jax.ShapeDtypeStruct((db, pool_w), F32)],
        compiler_params=_params(1),
        name="front_sample",
    )(*sample_in)

    x1_p, olat_s = _attention_with_paged(
        qlat_p, qrope_p, ctbf_p, krtbf_p, cbf_p, x_prompt, py_p, wuv, wout,
        page_table, jnp.swapaxes(qlat_s, 0, 1), jnp.swapaxes(qrope_s, 0, 1),
        c_s.reshape(db, 1, kv_lora), kr_s.reshape(db, 1, rope),
        cache_kv_latent[l], jnp.swapaxes(cache_k_rope[l], 1, 2),
        tq=_tile(seq, 512))

    mem_spec = pl.BlockSpec((1, n_mem, d), lambda b: (b, 0, 0))
    mk_p, mv_p, mkbf_p, mvbf_p = pl.pallas_call(
        _mem_kv_kernel,
        grid=(bsz,),
        in_specs=[mem_spec, _const_spec(gmem.shape), _const_spec(wmk.shape), _const_spec(wmv.shape)],
        out_specs=[mem_spec] * 4,
        out_shape=[jax.ShapeDtypeStruct((bsz, n_mem, d), F32)] * 2
        + [jax.ShapeDtypeStruct((bsz, n_mem, d), BF16)] * 2,
        compiler_params=_params(1),
        name="mem_kv",
    )(mem_prompt, gmem, wmk, wmv)

    tmem = _tile(seq, 512)
    xm_spec = pl.BlockSpec((1, tmem, d), lambda b, i: (b, i, 0))
    memkv_spec = pl.BlockSpec((1, n_mem, d), lambda b, i: (b, 0, 0))
    x2_p = pl.pallas_call(
        functools.partial(_mem_attn_prompt_kernel, heads=mem_heads, scale=mem_scale),
        grid=(bsz, seq // tmem),
        in_specs=[xm_spec, _const_spec(gx.shape), _const_spec(wxq.shape), memkv_spec, memkv_spec,
                  _const_spec(wxo.shape)],
        out_specs=xm_spec,
        out_shape=jax.ShapeDtypeStruct((bsz, seq, d), F32),
        compiler_params=_params(2),
        name="mem_attn_prompt",
    )(x1_p, gx, wxq, mkbf_p, mvbf_p, wxo)

    mix_in = (jnp.swapaxes(olat_s, 0, 1), py_s, xs, wuv, wout)
    x1_s = pl.pallas_call(
        functools.partial(_mix_out_kernel, heads=heads),
        grid=(1,),
        in_specs=[_const_spec(a.shape) for a in mix_in],
        out_specs=_whole_spec((db, d)),
        out_shape=jax.ShapeDtypeStruct((db, d), F32),
        compiler_params=_params(1),
        name="mix_out_sample",
    )(*mix_in)

    assert mem_hd % LANES == 0
    slab = mem_heads * (mem_hd // LANES)

    def slab_view(a):
        a = a.reshape(db, n_mem, mem_heads, mem_hd // LANES, LANES)
        return jnp.transpose(a, (0, 1, 3, 2, 4)).reshape(db, n_mem * slab, LANES)

    y_prompt, x2_s = _ffn_with_mem_attention(
        x2_p.reshape(bsz * seq, d), gffn, wg, wu, wd, gfin,
        x1_s, gx, wxq, slab_view(cache_mem_k[l]), slab_view(cache_mem_v[l]), wxo,
        tm=_tile(bsz * seq, 256), heads=mem_heads, scale=mem_scale, n_mem=n_mem)
    y_prompt = y_prompt.reshape(bsz, seq, d)

    y_sample = _ffn(x2_s.reshape(1, db, d), gffn, wg, wu, wd, gfin, db).reshape(db, ts, d)

    mem_shape = (depth, bsz, n_mem, mem_heads, mem_hd)
    return (y_prompt, y_sample,
            c_p[None], jnp.swapaxes(krt_p, 1, 2)[None], pst_p[None, :, POOL_HALO - POOL_STATE:],
            mk_p.reshape(mem_shape), mv_p.reshape(mem_shape),
            c_s.reshape(depth, db, ts, kv_lora), kr_s.reshape(depth, db, ts, rope),
            jnp.concatenate([cache_pool[l][:, 1:], up_s[:, None]], 1)[None])
```

```python
import functools

import jax
import jax.numpy as jnp
import numpy as np
from jax import lax
from jax.experimental import pallas as pl
from jax.experimental.pallas import tpu as pltpu

F32 = jnp.float32
BF16 = jnp.bfloat16

EPS = 1e-6
ROPE_THETA = 10000.0
PAGE_SIZE = 128
POOL_WINDOWS = (2, 4, 8, 16)
POOL_STATE = max(POOL_WINDOWS) - 1
POOL_HALO = 16
LANES = 128
VMEM_LIMIT = 56 * 1024 * 1024
ATTN_ROWS = 256
ATTN_AHEAD = 3


def _rms(x, g):
    return x * lax.rsqrt(jnp.mean(x * x, -1, keepdims=True) + EPS) * g


def _dot(a, b):
    return jnp.dot(a, b, preferred_element_type=F32)


def _dot_nt(a, b):
    return lax.dot_general(a, b, (((1,), (1,)), ((), ())), preferred_element_type=F32)


def _lane_tile(x, n):
    return jnp.concatenate([x] * n, -1)


def _const_spec(shape):
    nd = len(shape)
    return pl.BlockSpec(shape, lambda *_: (0,) * nd, pipeline_mode=pl.Buffered(1))


def _whole_spec(shape):
    nd = len(shape)
    return pl.BlockSpec(shape, lambda *_: (0,) * nd)


def _params(n_axes):
    return pltpu.CompilerParams(dimension_semantics=("arbitrary",) * n_axes,
                                vmem_limit_bytes=VMEM_LIMIT)


def _mla_inputs(x, gmix, win, gq, wuq, gkv, wuk, cos, sin, scale, dims):
    q_lora, kv_lora, pool_w, heads, nope, rope = dims
    h = _rms(x, gmix).astype(BF16)
    u = _dot(h, win)
    i1, i2, i3 = q_lora, q_lora + kv_lora, q_lora + kv_lora + pool_w
    cq, ckv, up, krs = u[:, :i1], u[:, i1:i2], u[:, i2:i3], u[:, i3:]
    q = _dot(_rms(cq, gq).astype(BF16), wuq)
    n0 = heads * nope
    n1 = n0 + heads * rope
    q_rope = (q[:, n0:n1] * cos + q[:, n1:] * sin) * scale
    q_lat = [(_dot(q[:, hh * nope:(hh + 1) * nope].astype(BF16), wuk[hh]) * scale).astype(BF16)
             for hh in range(heads)]
    q_rope = [q_rope[:, hh * rope:(hh + 1) * rope].astype(BF16) for hh in range(heads)]
    c = _rms(ckv, gkv)
    k_rope = krs * cos[:, :2 * rope] + pltpu.roll(krs, rope, 1) * sin[:, :2 * rope]
    return q_lat, q_rope, c, k_rope, up


def _pool_project(sums, up, cnt, wpool, pscale):
    cg = wpool.shape[-1]
    ys = []
    for g in range(len(POOL_WINDOWS)):
        m = sums[g] / cnt[g] - up[:, g * cg:(g + 1) * cg]
        ys.append(_dot(m.astype(BF16), wpool[g]))
    return jnp.concatenate(ys, -1) * pscale


def _front_prompt_kernel(x_ref, gmix_ref, win_ref, gq_ref, wuq_ref, gkv_ref, wuk_ref, cos_ref, sin_ref,
                         wpool_ref, pscale_ref,
                         qlat_ref, qrope_ref, c_ref, krt_ref, cbf_ref, ctbf_ref, krtbf_ref, py_ref, pst_ref,
                         prev_ref, *, scale, dims, tm):
    t = pl.program_id(1)
    heads, rope = dims[3], dims[5]
    q_lat, q_rope, c, k_rope, up = _mla_inputs(
        x_ref[0], gmix_ref[...], win_ref[...], gq_ref[...], wuq_ref[...], gkv_ref[...], wuk_ref,
        cos_ref[...], sin_ref[...], scale, dims)
    for hh in range(heads):
        qlat_ref[0, hh] = q_lat[hh]
        qrope_ref[0, hh] = q_rope[hh]
    c_ref[0] = c
    cbf_ref[0] = c.astype(BF16)
    ctbf_ref[0] = c.T.astype(BF16)
    krt = k_rope.T[:rope]
    krt_ref[0] = krt
    krtbf_ref[0] = krt.astype(BF16)

    @pl.when(t == 0)
    def _():
        prev_ref[...] = jnp.zeros_like(prev_ref)

    cg = wpool_ref.shape[-1]
    e = jnp.concatenate([prev_ref[...], up], 0)
    tail = up[tm - POOL_HALO:, :]
    prev_ref[...] = tail
    pst_ref[0] = tail
    sums = []
    s = e
    for g, w in enumerate(POOL_WINDOWS):
        s = s[:, (cg if g else 0):]
        s = s + pltpu.roll(s, w // 2, 0)
        sums.append(s[POOL_HALO:, :cg])
    pos1 = t * tm + lax.broadcasted_iota(jnp.int32, (tm, cg), 0) + 1
    cnt = [jnp.minimum(w, pos1).astype(F32) for w in POOL_WINDOWS]
    py_ref[0] = _pool_project(sums, up, cnt, wpool_ref, pscale_ref[...]).astype(BF16)


def _front_sample_kernel(x_ref, gmix_ref, win_ref, gq_ref, wuq_ref, gkv_ref, wuk_ref, cos_ref, sin_ref,
                         wpool_ref, pscale_ref, prev_ref,
                         qlat_ref, qrope_ref, c_ref, kr_ref, py_ref, up_ref, *, scale, dims, past):
    heads, rope = dims[3], dims[5]
    q_lat, q_rope, c, k_rope, up = _mla_inputs(
        x_ref[...], gmix_ref[...], win_ref[...], gq_ref[...], wuq_ref[...], gkv_ref[...], wuk_ref,
        cos_ref[...], sin_ref[...], scale, dims)
    for hh in range(heads):
        qlat_ref[hh] = q_lat[hh]
        qrope_ref[hh] = q_rope[hh]
    c_ref[...] = c
    kr_ref[...] = k_rope[:, :rope]
    up_ref[...] = up

    cg = wpool_ref.shape[-1]
    sums, cnt = [], []
    for g, w in enumerate(POOL_WINDOWS):
        s = up[:, g * cg:(g + 1) * cg]
        for j in range(1, w):
            s = s + prev_ref[POOL_STATE - j][:, g * cg:(g + 1) * cg]
        sums.append(s)
        cnt.append(float(min(w, past + 1)))
    py_ref[...] = _pool_project(sums, up, cnt, wpool_ref, pscale_ref[...]).astype(BF16)


def _mix_out(o_lat_heads, py, x, wuv_ref, wout_ref):
    o_mla = jnp.concatenate(
        [_dot(o.astype(BF16), wuv_ref[hh]) for hh, o in enumerate(o_lat_heads)], -1).astype(BF16)
    wm = o_mla.shape[-1]
    return x + _dot(o_mla, wout_ref[:wm, :]) + _dot(py, wout_ref[wm:, :])


def _paged_attention(ql, qr, cn, krn, cbf, rot, pieces):
    cnr = cn.astype(BF16).astype(F32)
    krnr = krn.astype(BF16).astype(F32)
    w = cbf.shape[0] // pieces

    def scores(k):
        return (_dot_nt(ql, cbf[k * w:(k + 1) * w, :])
                + _dot(qr, rot[:, k * w:(k + 1) * w].astype(BF16)))

    def softmax(s_parts):
        s = jnp.concatenate(s_parts, -1)
        s_new = (jnp.sum(ql.astype(F32) * cnr, -1, keepdims=True)
                 + jnp.sum(qr.astype(F32) * krnr, -1, keepdims=True))
        m = jnp.maximum(jnp.max(s, -1, keepdims=True), s_new)
        p = jnp.exp(s - m)
        p_new = jnp.exp(s_new - m)
        return p.astype(BF16), jnp.sum(p, -1, keepdims=True) + p_new, p_new

    def values(p, k):
        return _dot(p[:, k * w:(k + 1) * w], cbf[k * w:(k + 1) * w, :])

    def finish(v_parts, l, p_new):
        return (sum(v_parts) + p_new.astype(BF16).astype(F32) * cnr) / l

    return scores, softmax, values, finish


def _attn_prompt_kernel(tab_ref, pt_ref,
                        ql_ref, qr_ref, kt_ref, rt_ref, v_ref, x_ref, py_ref, wuv_ref, wout_ref,
                        sql_ref, sqr_ref, cn_ref, krn_ref, cache_c, cache_rt,
                        o_ref, so_ref,
                        m_ref, l_ref, acc_ref, kbuf, rbuf, cbf, sem,
                        *, tq, heads, n_pages, n_req, n_steps):
    step = pl.program_id(0)
    i = tab_ref[n_steps + step]
    j = tab_ref[2 * n_steps + step]
    req = tab_ref[3 * n_steps + step]
    tk = tq
    chains = [(hh, r) for hh in range(heads) for r in range(tq // ATTN_ROWS)]

    def page_copies(rq, sl):
        lat, rot = [], []
        for p in range(n_pages):
            page = pt_ref[rq * n_pages + p]
            toks = pl.ds(p * PAGE_SIZE, PAGE_SIZE)
            lat.append(pltpu.make_async_copy(cache_c.at[page], kbuf.at[sl, toks, :], sem.at[0, sl]))
            rot.append(pltpu.make_async_copy(cache_rt.at[page], rbuf.at[sl, :, toks], sem.at[1, sl]))
        return lat + rot

    def start_pages(rq, sl):
        for k, cp in enumerate(page_copies(rq, sl)):
            cp.start(priority=k % 2)

    @pl.when(step == 0)
    def _():
        start_pages(0, 0)

    @pl.when(j == 0)
    def _():
        m_ref[...] = jnp.full_like(m_ref, -jnp.inf)
        l_ref[...] = jnp.zeros_like(l_ref)
        acc_ref[...] = jnp.zeros_like(acc_ref)

    def attend(diagonal, with_sample):
        if with_sample:
            sl = req % 2
            for cp in page_copies(req, sl):
                cp.wait()
            nxt = jnp.minimum(req + 1, n_req - 1)
            start_pages(nxt, 1 - sl)

            cbf[...] = kbuf[sl].astype(BF16)
            pieces = len(chains) // 2
            s_scores, s_softmax, s_values, s_finish = _paged_attention(
                sql_ref[req], sqr_ref[req], cn_ref[req], krn_ref[req], cbf, rbuf.at[sl], pieces)
            s_parts, v_parts = [], []

        off = pl.multiple_of(j * tk, tk)
        kt = kt_ref[0, :, pl.ds(off, tk)]
        rt = rt_ref[0, :, pl.ds(off, tk)]
        v = v_ref[0, pl.ds(off, tk), :]

        def width(r):
            return (r + 1) * ATTN_ROWS if diagonal else tk

        def scores(ci):
            hh, r = chains[ci]
            rows = pl.ds(r * ATTN_ROWS, ATTN_ROWS)
            w = width(r)
            return _dot(ql_ref[0, hh, rows, :], kt[:, :w]) + _dot(qr_ref[0, hh, rows, :], rt[:, :w])

        pending = [scores(ci) for ci in range(ATTN_AHEAD)]
        for ci, (hh, r) in enumerate(chains):
            if ci + ATTN_AHEAD < len(chains):
                pending.append(scores(ci + ATTN_AHEAD))
            if with_sample and ci < pieces:
                s_parts.append(s_scores(ci))
            if with_sample and ci == pieces:
                p_sample, l_sample, p_new = s_softmax(s_parts)
            if with_sample and ci >= pieces:
                v_parts.append(s_values(p_sample, ci - pieces))
            s = pending[ci]
            if diagonal:
                qpos = r * ATTN_ROWS + lax.broadcasted_iota(jnp.int32, s.shape, 0)
                kpos = lax.broadcasted_iota(jnp.int32, s.shape, 1)
                s = jnp.where(kpos <= qpos, s, -jnp.inf)
            m_prev = m_ref[ci]
            m_new = jnp.maximum(m_prev, jnp.max(s, -1, keepdims=True))
            alpha = jnp.exp(m_prev - m_new)
            p = jnp.exp(s - _lane_tile(m_new, width(r) // LANES))
            l_ref[ci] = alpha * l_ref[ci] + jnp.sum(p, -1, keepdims=True)
            acc_ref[ci] = (_lane_tile(alpha, acc_ref.shape[-1] // LANES) * acc_ref[ci]
                           + _dot(p.astype(BF16), v[:width(r)]))
            m_ref[ci] = m_new
        if with_sample:
            so_ref[req] = s_finish(v_parts, l_sample, p_new)

            @pl.when(req == n_req - 1)
            def _():
                for cp in page_copies(nxt, 1 - sl):
                    cp.wait()

    for diagonal in (False, True):
        for with_sample in (False, True):
            on_diag = (j == i) if diagonal else (j < i)
            hosting = (req >= 0) if with_sample else (req < 0)
            pl.when(on_diag & hosting)(functools.partial(attend, diagonal, with_sample))

    @pl.when(j == i)
    def _():
        nr = tq // ATTN_ROWS
        rep = acc_ref.shape[-1] // LANES
        o_heads = [jnp.concatenate([acc_ref[hh * nr + r] / _lane_tile(l_ref[hh * nr + r], rep)
                                    for r in range(nr)], 0)
                   for hh in range(heads)]
        o_ref[0] = _mix_out(o_heads, py_ref[0], x_ref[0], wuv_ref, wout_ref)


def _mem_kv_kernel(mem_ref, g_ref, wk_ref, wv_ref, k_ref, v_ref, kbf_ref, vbf_ref):
    m = _rms(mem_ref[0], g_ref[...]).astype(BF16)
    k = _dot(m, wk_ref[...])
    v = _dot(m, wv_ref[...])
    k_ref[0] = k
    v_ref[0] = v
    kbf_ref[0] = k.astype(BF16)
    vbf_ref[0] = v.astype(BF16)


def _mem_attn_prompt_kernel(x_ref, g_ref, wq_ref, k_ref, v_ref, wo_ref, o_ref, *, heads, scale):
    x = x_ref[0]
    q = (_dot(_rms(x, g_ref[...]).astype(BF16), wq_ref[...]) * scale).astype(BF16)
    hd = q.shape[-1] // heads
    outs = []
    for hh in range(heads):
        sl = slice(hh * hd, (hh + 1) * hd)
        s = _dot_nt(q[:, sl], k_ref[0, :, sl])
        p = jnp.exp(s - jnp.max(s, -1, keepdims=True))
        o = _dot(p.astype(BF16), v_ref[0, :, sl]) / jnp.sum(p, -1, keepdims=True)
        outs.append(o.astype(BF16))
    o_ref[0] = x + _dot(jnp.concatenate(outs, -1), wo_ref[...])


def _slab_col(r, heads, rr):
    return (r % heads) * (rr // heads) + r // heads


def _mem_query_slabs(x, g, wq_ref, q_ref, heads, scale):
    db = x.shape[0]
    rr = q_ref.shape[0] // db
    hq = _rms(x, g).astype(BF16)
    for r in range(rr):
        j = _slab_col(r, heads, rr)
        q_ref[pl.ds(r, db, stride=rr), :] = _dot(hq, wq_ref[:, j * LANES:(j + 1) * LANES]) * scale


def _mem_attend_request(q_ref, a_ref, k, v, request, heads, n_mem):
    rr = k.shape[0] // n_mem
    row = pl.multiple_of(request * rr, rr)
    qv = q_ref[pl.ds(row, rr), :]
    part = jnp.sum(k.reshape(n_mem, rr, LANES) * qv[None], -1, keepdims=True)
    s = part
    for d in range(1, rr // heads):
        s = s + pltpu.roll(part, d * heads, 1)
    p = jnp.exp(s - jnp.max(s, 0, keepdims=True))
    l = jnp.sum(p, 0)
    a_ref[pl.ds(row, rr), :] = jnp.sum(p * v.reshape(n_mem, rr, LANES), 0) / l


def _mem_output(x, a_ref, wo_ref, heads):
    db = x.shape[0]
    rr = a_ref.shape[0] // db
    acc = x
    for r in range(rr):
        j = _slab_col(r, heads, rr)
        a = a_ref[pl.ds(r, db, stride=rr), :].astype(BF16)
        acc = acc + _dot(a, wo_ref[j * LANES:(j + 1) * LANES, :])
    return acc


def _swiglu_final(x, g, wg_ref, wu_ref, wd_ref, gf):
    h = _rms(x, g).astype(BF16)
    gate = _dot(h, wg_ref[...])
    a = (gate * jax.nn.sigmoid(gate) * _dot(h, wu_ref[...])).astype(BF16)
    return _rms(x + _dot(a, wd_ref[...]), gf)


def _ffn_mem_kernel(x_ref, g_ref, wg_ref, wu_ref, wd_ref, gf_ref,
                    ol_ref, py_ref, xs_ref, wuv_ref, wout_ref, gx_ref, wq_ref, k_ref, v_ref, wo_ref,
                    y_ref, ys_ref, x1_ref, q_ref, a_ref, *, mla_heads, heads, scale, nb, n_mem):
    s_id = pl.program_id(0)

    @pl.when(s_id == 0)
    def _():
        x1 = _mix_out([ol_ref[hh] for hh in range(mla_heads)], py_ref[...], xs_ref[...], wuv_ref, wout_ref)
        x1_ref[...] = x1
        _mem_query_slabs(x1, gx_ref[...], wq_ref, q_ref, heads, scale)

    y_ref[...] = _swiglu_final(x_ref[...], g_ref[...], wg_ref, wu_ref, wd_ref, gf_ref[...])
    for bb in range(nb):
        _mem_attend_request(q_ref, a_ref, k_ref[bb], v_ref[bb], s_id * nb + bb, heads, n_mem)

    @pl.when(s_id == pl.num_programs(0) - 1)
    def _():
        x2 = _mem_output(x1_ref[...], a_ref, wo_ref, heads)
        ys_ref[...] = _swiglu_final(x2, g_ref[...], wg_ref, wu_ref, wd_ref, gf_ref[...])


def _rope_tables(pos, rope, heads):
    half = rope // 2
    inv = jnp.power(ROPE_THETA, -(jnp.arange(half, dtype=F32) / half))
    ang = pos.astype(F32)[:, None] * inv[None, :]
    cos, sin = jnp.cos(ang), jnp.sin(ang)
    return (jnp.tile(jnp.concatenate([cos, cos], -1), (1, heads)),
            jnp.tile(jnp.concatenate([-sin, sin], -1), (1, heads)))


def _swap_halves(w):
    half = w.shape[-1] // 2
    return jnp.concatenate([w[..., half:], w[..., :half]], -1)


def _row(v):
    return v.reshape(1, -1)


def _tile(n, pref):
    return pref if n % pref == 0 else n


def _attention_schedule(bsz, n_tiles, n_req):
    steps = [(b, i, j) for b in range(bsz) for i in range(n_tiles) for j in range(i + 1)]
    n_steps = len(steps)
    assert n_req <= n_steps, "at most one sample request per attention step"
    by_cost = sorted(range(n_steps), key=lambda s: (steps[s][1] == steps[s][2], steps[s][1]), reverse=True)
    idle = set(by_cost[:n_steps - n_req])
    assert 0 not in idle, "the first step starts the page stream"
    tab = np.zeros((4, n_steps), np.int32)
    nxt = 0
    for s, (b, i, j) in enumerate(steps):
        hosted = s not in idle
        tab[:, s] = (b, i, j, nxt if hosted else -1)
        nxt += hosted
    return tab


def _attention_with_paged(qlat_p, qrope_p, ctbf, krtbf, cbf, x, py, wuv, wout,
                          page_table, qlat_s, qrope_s, c_new, kr_new, cache_c, cache_rt, tq):
    bsz, heads, seq, kv = qlat_p.shape
    rope = qrope_p.shape[-1]
    d = x.shape[-1]
    db = qlat_s.shape[0]
    n_pages = page_table.shape[1]
    past = n_pages * PAGE_SIZE
    assert seq % tq == 0 and tq % ATTN_ROWS == 0
    n_chain = heads * (tq // ATTN_ROWS)
    tab = _attention_schedule(bsz, seq // tq, db)
    n_steps = tab.shape[1]
    col = lambda r: (lambda s, tab_ref, pt: tab_ref[r * n_steps + s])
    bb, ii = col(0), col(1)
    q_spec = lambda w: pl.BlockSpec((1, heads, tq, w), lambda *a: (bb(*a), 0, ii(*a), 0))
    batch_spec = lambda r, c: pl.BlockSpec((1, r, c), lambda *a: (bb(*a), 0, 0), pipeline_mode=pl.Buffered(1))
    row_spec = lambda w: pl.BlockSpec((1, tq, w), lambda *a: (bb(*a), ii(*a), 0))
    return pl.pallas_call(
        functools.partial(_attn_prompt_kernel, tq=tq, heads=heads, n_pages=n_pages, n_req=db, n_steps=n_steps),
        grid_spec=pltpu.PrefetchScalarGridSpec(
            num_scalar_prefetch=2,
            grid=(n_steps,),
            in_specs=[q_spec(kv), q_spec(rope), batch_spec(kv, seq), batch_spec(rope, seq), batch_spec(seq, kv),
                      row_spec(d), row_spec(py.shape[-1]), _const_spec(wuv.shape), _const_spec(wout.shape),
                      _const_spec(qlat_s.shape), _const_spec(qrope_s.shape), _const_spec(c_new.shape),
                      _const_spec(kr_new.shape),
                      pl.BlockSpec(memory_space=pl.ANY), pl.BlockSpec(memory_space=pl.ANY)],
            out_specs=[row_spec(d), _whole_spec((db, heads, kv))],
            scratch_shapes=[pltpu.VMEM((n_chain, ATTN_ROWS, LANES), F32),
                            pltpu.VMEM((n_chain, ATTN_ROWS, LANES), F32),
                            pltpu.VMEM((n_chain, ATTN_ROWS, kv), F32),
                            pltpu.VMEM((2, past, kv), F32), pltpu.VMEM((2, rope, past), F32),
                            pltpu.VMEM((past, kv), BF16), pltpu.SemaphoreType.DMA((2, 2))],
        ),
        out_shape=[jax.ShapeDtypeStruct((bsz, seq, d), F32), jax.ShapeDtypeStruct((db, heads, kv), F32)],
        compiler_params=_params(1),
        name="attn_prompt_paged",
    )(jnp.asarray(tab.reshape(-1)), page_table.reshape(-1),
      qlat_p, qrope_p, ctbf, krtbf, cbf, x, py, wuv, wout,
      qlat_s, qrope_s, c_new, kr_new, cache_c, cache_rt)


def _ffn_with_sample_tail(x, g, wg, wu, wd, gf, olat, py, xs, wuv, wout, gx, wxq, mem_k, mem_v, wxo,
                          tm, heads, scale, n_mem):
    rows, d = x.shape
    db = xs.shape[0]
    n_steps = rows // tm
    assert rows % tm == 0 and db % n_steps == 0
    nb = db // n_steps
    slab = mem_k.shape[1] // n_mem
    x_spec = pl.BlockSpec((tm, d), lambda s: (s, 0))
    cache_spec = pl.BlockSpec((nb,) + mem_k.shape[1:], lambda s: (s, 0, 0))
    resident = (g, wg, wu, wd, gf, olat, py, xs, wuv, wout, gx, wxq)
    return pl.pallas_call(
        functools.partial(_ffn_mem_kernel, mla_heads=olat.shape[0], heads=heads, scale=scale, nb=nb,
                          n_mem=n_mem),
        grid=(n_steps,),
        in_specs=[x_spec] + [_const_spec(a.shape) for a in resident]
        + [cache_spec, cache_spec, _const_spec(wxo.shape)],
        out_specs=[x_spec, _whole_spec((db, d))],
        out_shape=[jax.ShapeDtypeStruct((rows, d), F32), jax.ShapeDtypeStruct((db, d), F32)],
        scratch_shapes=[pltpu.VMEM((db, d), F32), pltpu.VMEM((db * slab, LANES), F32),
                        pltpu.VMEM((db * slab, LANES), F32)],
        compiler_params=_params(1),
        name="ffn_mem",
    )(x, *resident, mem_k, mem_v, wxo)


def kernel(x_prompt, x_sample, mem_prompt, cache_kv_latent, cache_k_rope, cache_pool, cache_mem_k, cache_mem_v, page_table, norm_mix, w_in, q_norm, w_uq, kv_norm, w_uk, w_uv, w_pool, pool_scale, w_out, norm_x, mem_norm, w_xq, w_mk, w_mv, w_xo, norm_ffn, w_gate, w_up, w_down, norm_final):
    depth = w_in.shape[0]
    assert depth == 1, "single-layer trunk"
    bsz, seq, d = x_prompt.shape
    db, ts, _ = x_sample.shape
    assert ts == 1, "one new token per request"
    q_lora, kv_lora = q_norm.shape[1], kv_norm.shape[1]
    heads, nope = w_uk.shape[2], w_uk.shape[3]
    rope = cache_k_rope.shape[-1]
    pool_w = pool_scale.shape[1]
    v_head = w_uv.shape[3]
    n_mem, mem_heads, mem_hd = cache_mem_k.shape[2], cache_mem_k.shape[3], cache_mem_k.shape[4]
    n_pages = page_table.shape[1]
    assert cache_kv_latent.shape[2] == PAGE_SIZE
    past = n_pages * PAGE_SIZE
    mla_scale = float((nope + rope) ** -0.5)
    mem_scale = float(mem_hd ** -0.5)
    dims = (q_lora, kv_lora, pool_w, heads, nope, rope)
    l = 0

    i1, i2, i3 = q_lora, q_lora + kv_lora, q_lora + kv_lora + rope
    wi = w_in[l]
    w_kr = wi[:, i2:i3]
    win = jnp.concatenate([wi[:, :i2], wi[:, i3:], w_kr, _swap_halves(w_kr)], -1).astype(BF16)
    wq3 = w_uq[l].reshape(q_lora, heads, nope + rope)
    wq_rope = wq3[:, :, nope:]
    wuq = jnp.concatenate([wq3[:, :, :nope].reshape(q_lora, heads * nope),
                           wq_rope.reshape(q_lora, heads * rope),
                           _swap_halves(wq_rope).reshape(q_lora, heads * rope)], -1).astype(BF16)
    wuk = jnp.transpose(w_uk[l], (1, 2, 0)).astype(BF16)
    wuv = jnp.transpose(w_uv[l], (1, 0, 2)).astype(BF16)
    wpool = w_pool[l].astype(BF16)
    wout = w_out[l].astype(BF16)
    wxq, wxo = w_xq[l].astype(BF16), w_xo[l].astype(BF16)
    wmk, wmv = w_mk[l].astype(BF16), w_mv[l].astype(BF16)
    wg, wu, wd = w_gate[l].astype(BF16), w_up[l].astype(BF16), w_down[l].astype(BF16)
    gmix, gq, gkv = _row(norm_mix[l]), _row(q_norm[l]), _row(kv_norm[l])
    gx, gmem, gffn, gfin = _row(norm_x[l]), _row(mem_norm[l]), _row(norm_ffn[l]), _row(norm_final)
    pscale = _row(pool_scale[l])
    front_consts = (gmix, win, gq, wuq, gkv, wuk)
    hr = heads * rope

    tm = _tile(seq, 512)
    cos_p, sin_p = _rope_tables(jnp.arange(seq), rope, heads)
    row_spec = lambda w: pl.BlockSpec((1, tm, w), lambda b, t: (b, t, 0))
    head_spec = lambda w: pl.BlockSpec((1, heads, tm, w), lambda b, t: (b, 0, t, 0))
    tab_spec = pl.BlockSpec((tm, hr), lambda b, t: (t, 0))
    col_spec = lambda w: pl.BlockSpec((1, w, tm), lambda b, t: (b, 0, t))
    qlat_p, qrope_p, c_p, krt_p, cbf_p, ctbf_p, krtbf_p, py_p, pst_p = pl.pallas_call(
        functools.partial(_front_prompt_kernel, scale=mla_scale, dims=dims, tm=tm),
        grid=(bsz, seq // tm),
        in_specs=[row_spec(d)] + [_const_spec(a.shape) for a in front_consts]
        + [tab_spec, tab_spec, _const_spec(wpool.shape), _const_spec(pscale.shape)],
        out_specs=[head_spec(kv_lora), head_spec(rope), row_spec(kv_lora), col_spec(rope),
                   row_spec(kv_lora), col_spec(kv_lora), col_spec(rope), row_spec(pool_w),
                   pl.BlockSpec((1, POOL_HALO, pool_w), lambda b, t: (b, 0, 0))],
        out_shape=[jax.ShapeDtypeStruct((bsz, heads, seq, kv_lora), BF16),
                   jax.ShapeDtypeStruct((bsz, heads, seq, rope), BF16),
                   jax.ShapeDtypeStruct((bsz, seq, kv_lora), F32),
                   jax.ShapeDtypeStruct((bsz, rope, seq), F32),
                   jax.ShapeDtypeStruct((bsz, seq, kv_lora), BF16),
                   jax.ShapeDtypeStruct((bsz, kv_lora, seq), BF16),
                   jax.ShapeDtypeStruct((bsz, rope, seq), BF16),
                   jax.ShapeDtypeStruct((bsz, seq, pool_w), BF16),
                   jax.ShapeDtypeStruct((bsz, POOL_HALO, pool_w), F32)],
        scratch_shapes=[pltpu.VMEM((POOL_HALO, pool_w), F32)],
        compiler_params=_params(2),
        name="front_prompt",
    )(x_prompt, *front_consts, cos_p, sin_p, wpool, pscale)

    xs = x_sample.reshape(db, d)
    cos_s, sin_s = _rope_tables(jnp.full((db,), past), rope, heads)
    prev_s = jnp.transpose(cache_pool[l], (1, 0, 2))
    sample_in = (xs, *front_consts, cos_s, sin_s, wpool, pscale, prev_s)
    qlat_s, qrope_s, c_s, kr_s, py_s, up_s = pl.pallas_call(
        functools.partial(_front_sample_kernel, scale=mla_scale, dims=dims, past=past),
        grid=(1,),
        in_specs=[_const_spec(a.shape) for a in sample_in],
        out_specs=[_whole_spec(s) for s in ((heads, db, kv_lora), (heads, db, rope), (db, kv_lora),
                                            (db, rope), (db, pool_w), (db, pool_w))],
        out_shape=[jax.ShapeDtypeStruct((heads, db, kv_lora), BF16),
                   jax.ShapeDtypeStruct((heads, db, rope), BF16),
                   jax.ShapeDtypeStruct((db, kv_lora), F32),
                   jax.ShapeDtypeStruct((db, rope), F32),
                   jax.ShapeDtypeStruct((db, pool_w), BF16),
                   jax.ShapeDtypeStruct((db, pool_w), F32)],
        compiler_params=_params(1),
        name="front_sample",
    )(*sample_in)

    x1_p, olat_s = _attention_with_paged(
        qlat_p, qrope_p, ctbf_p, krtbf_p, cbf_p, x_prompt, py_p, wuv, wout,
        page_table, jnp.swapaxes(qlat_s, 0, 1), jnp.swapaxes(qrope_s, 0, 1),
        c_s.reshape(db, 1, kv_lora), kr_s.reshape(db, 1, rope),
        cache_kv_latent[l], jnp.swapaxes(cache_k_rope[l], 1, 2),
        tq=_tile(seq, 512))

    mem_spec = pl.BlockSpec((1, n_mem, d), lambda b: (b, 0, 0))
    mk_p, mv_p, mkbf_p, mvbf_p = pl.pallas_call(
        _mem_kv_kernel,
        grid=(bsz,),
        in_specs=[mem_spec, _const_spec(gmem.shape), _const_spec(wmk.shape), _const_spec(wmv.shape)],
        out_specs=[mem_spec] * 4,
        out_shape=[jax.ShapeDtypeStruct((bsz, n_mem, d), F32)] * 2
        + [jax.ShapeDtypeStruct((bsz, n_mem, d), BF16)] * 2,
        compiler_params=_params(1),
        name="mem_kv",
    )(mem_prompt, gmem, wmk, wmv)

    tmem = _tile(seq, 512)
    xm_spec = pl.BlockSpec((1, tmem, d), lambda b, i: (b, i, 0))
    memkv_spec = pl.BlockSpec((1, n_mem, d), lambda b, i: (b, 0, 0))
    x2_p = pl.pallas_call(
        functools.partial(_mem_attn_prompt_kernel, heads=mem_heads, scale=mem_scale),
        grid=(bsz, seq // tmem),
        in_specs=[xm_spec, _const_spec(gx.shape), _const_spec(wxq.shape), memkv_spec, memkv_spec,
                  _const_spec(wxo.shape)],
        out_specs=xm_spec,
        out_shape=jax.ShapeDtypeStruct((bsz, seq, d), F32),
        compiler_params=_params(2),
        name="mem_attn_prompt",
    )(x1_p, gx, wxq, mkbf_p, mvbf_p, wxo)

    assert mem_hd % LANES == 0
    slab = mem_heads * (mem_hd // LANES)

    def slab_view(a):
        a = a.reshape(db, n_mem, mem_heads, mem_hd // LANES, LANES)
        return jnp.transpose(a, (0, 1, 3, 2, 4)).reshape(db, n_mem * slab, LANES)

    y_prompt, y_sample = _ffn_with_sample_tail(
        x2_p.reshape(bsz * seq, d), gffn, wg, wu, wd, gfin,
        jnp.swapaxes(olat_s, 0, 1), py_s, xs, wuv, wout,
        gx, wxq, slab_view(cache_mem_k[l]), slab_view(cache_mem_v[l]), wxo,
        tm=_tile(bsz * seq, 256), heads=mem_heads, scale=mem_scale, n_mem=n_mem)
    y_prompt = y_prompt.reshape(bsz, seq, d)
    y_sample = y_sample.reshape(db, ts, d)

    mem_shape = (depth, bsz, n_mem, mem_heads, mem_hd)
    return (y_prompt, y_sample,
            c_p[None], jnp.swapaxes(krt_p, 1, 2)[None], pst_p[None, :, POOL_HALO - POOL_STATE:],
            mk_p.reshape(mem_shape), mv_p.reshape(mem_shape),
            c_s.reshape(depth, db, ts, kv_lora), kr_s.reshape(depth, db, ts, rope),
            jnp.concatenate([cache_pool[l][:, 1:], up_s[:, None]], 1)[None])
```

```python
import functools

import jax
import jax.numpy as jnp
import numpy as np
from jax import lax
from jax.experimental import pallas as pl
from jax.experimental.pallas import tpu as pltpu

F32 = jnp.float32
BF16 = jnp.bfloat16

EPS = 1e-6
ROPE_THETA = 10000.0
PAGE_SIZE = 128
POOL_WINDOWS = (2, 4, 8, 16)
POOL_STATE = max(POOL_WINDOWS) - 1
POOL_HALO = 16
LANES = 128
VMEM_LIMIT = 56 * 1024 * 1024
ATTN_ROWS = 256
ATTN_AHEAD = 3


def _rms(x, g):
    return x * lax.rsqrt(jnp.mean(x * x, -1, keepdims=True) + EPS) * g


def _dot(a, b):
    return jnp.dot(a, b, preferred_element_type=F32)


def _dot_nt(a, b):
    return lax.dot_general(a, b, (((1,), (1,)), ((), ())), preferred_element_type=F32)


def _lane_tile(x, n):
    return jnp.concatenate([x] * n, -1)


def _const_spec(shape):
    nd = len(shape)
    return pl.BlockSpec(shape, lambda *_: (0,) * nd, pipeline_mode=pl.Buffered(1))


def _whole_spec(shape):
    nd = len(shape)
    return pl.BlockSpec(shape, lambda *_: (0,) * nd)


def _params(n_axes):
    return pltpu.CompilerParams(dimension_semantics=("arbitrary",) * n_axes,
                                vmem_limit_bytes=VMEM_LIMIT)


def _mla_inputs(x, gmix, win, gq, wuq, gkv, wuk, cos, sin, scale, dims):
    q_lora, kv_lora, pool_w, heads, nope, rope = dims
    h = _rms(x, gmix).astype(BF16)
    u = _dot(h, win)
    i1, i2, i3 = q_lora, q_lora + kv_lora, q_lora + kv_lora + pool_w
    cq, ckv, up, krs = u[:, :i1], u[:, i1:i2], u[:, i2:i3], u[:, i3:]
    q = _dot(_rms(cq, gq).astype(BF16), wuq)
    n0 = heads * nope
    n1 = n0 + heads * rope
    q_rope = (q[:, n0:n1] * cos + q[:, n1:] * sin) * scale
    q_lat = [(_dot(q[:, hh * nope:(hh + 1) * nope].astype(BF16), wuk[hh]) * scale).astype(BF16)
             for hh in range(heads)]
    q_rope = [q_rope[:, hh * rope:(hh + 1) * rope].astype(BF16) for hh in range(heads)]
    c = _rms(ckv, gkv)
    k_rope = krs * cos[:, :2 * rope] + pltpu.roll(krs, rope, 1) * sin[:, :2 * rope]
    return q_lat, q_rope, c, k_rope, up


def _pool_project(sums, up, cnt, wpool, pscale):
    cg = wpool.shape[-1]
    ys = []
    for g in range(len(POOL_WINDOWS)):
        m = sums[g] / cnt[g] - up[:, g * cg:(g + 1) * cg]
        ys.append(_dot(m.astype(BF16), wpool[g]))
    return jnp.concatenate(ys, -1) * pscale


def _front_prompt_kernel(x_ref, gmix_ref, win_ref, gq_ref, wuq_ref, gkv_ref, wuk_ref, cos_ref, sin_ref,
                         wpool_ref, pscale_ref,
                         qlat_ref, qrope_ref, c_ref, krt_ref, cbf_ref, ctbf_ref, krtbf_ref, py_ref, pst_ref,
                         prev_ref, *, scale, dims, tm):
    t = pl.program_id(1)
    heads, rope = dims[3], dims[5]
    q_lat, q_rope, c, k_rope, up = _mla_inputs(
        x_ref[0], gmix_ref[...], win_ref[...], gq_ref[...], wuq_ref[...], gkv_ref[...], wuk_ref,
        cos_ref[...], sin_ref[...], scale, dims)
    for hh in range(heads):
        qlat_ref[0, hh] = q_lat[hh]
        qrope_ref[0, hh] = q_rope[hh]
    c_ref[0] = c
    cbf_ref[0] = c.astype(BF16)
    ctbf_ref[0] = c.T.astype(BF16)
    krt = k_rope.T[:rope]
    krt_ref[0] = krt
    krtbf_ref[0] = krt.astype(BF16)

    @pl.when(t == 0)
    def _():
        prev_ref[...] = jnp.zeros_like(prev_ref)

    cg = wpool_ref.shape[-1]
    e = jnp.concatenate([prev_ref[...], up], 0)
    tail = up[tm - POOL_HALO:, :]
    prev_ref[...] = tail
    pst_ref[0] = tail
    sums = []
    s = e
    for g, w in enumerate(POOL_WINDOWS):
        s = s[:, (cg if g else 0):]
        s = s + pltpu.roll(s, w // 2, 0)
        sums.append(s[POOL_HALO:, :cg])
    pos1 = t * tm + lax.broadcasted_iota(jnp.int32, (tm, cg), 0) + 1
    cnt = [jnp.minimum(w, pos1).astype(F32) for w in POOL_WINDOWS]
    py_ref[0] = _pool_project(sums, up, cnt, wpool_ref, pscale_ref[...]).astype(BF16)


def _front_sample_kernel(x_ref, gmix_ref, win_ref, gq_ref, wuq_ref, gkv_ref, wuk_ref, cos_ref, sin_ref,
                         wpool_ref, pscale_ref, prev_ref,
                         qlat_ref, qrope_ref, c_ref, kr_ref, py_ref, up_ref, *, scale, dims, past):
    heads, rope = dims[3], dims[5]
    q_lat, q_rope, c, k_rope, up = _mla_inputs(
        x_ref[...], gmix_ref[...], win_ref[...], gq_ref[...], wuq_ref[...], gkv_ref[...], wuk_ref,
        cos_ref[...], sin_ref[...], scale, dims)
    for hh in range(heads):
        qlat_ref[hh] = q_lat[hh]
        qrope_ref[hh] = q_rope[hh]
    c_ref[...] = c
    kr_ref[...] = k_rope[:, :rope]
    up_ref[...] = up

    cg = wpool_ref.shape[-1]
    sums, cnt = [], []
    for g, w in enumerate(POOL_WINDOWS):
        s = up[:, g * cg:(g + 1) * cg]
        for j in range(1, w):
            s = s + prev_ref[POOL_STATE - j][:, g * cg:(g + 1) * cg]
        sums.append(s)
        cnt.append(float(min(w, past + 1)))
    py_ref[...] = _pool_project(sums, up, cnt, wpool_ref, pscale_ref[...]).astype(BF16)


def _mix_out(o_lat_heads, py, x, wuv_ref, wout_ref):
    o_mla = jnp.concatenate(
        [_dot(o.astype(BF16), wuv_ref[hh]) for hh, o in enumerate(o_lat_heads)], -1).astype(BF16)
    wm = o_mla.shape[-1]
    return x + _dot(o_mla, wout_ref[:wm, :]) + _dot(py, wout_ref[wm:, :])


def _paged_attention(ql, qr, cn, krn, cbf, rot, pieces):
    cnr = cn.astype(BF16).astype(F32)
    krnr = krn.astype(BF16).astype(F32)
    w = cbf.shape[0] // pieces

    def scores(k):
        return (_dot_nt(ql, cbf[k * w:(k + 1) * w, :])
                + _dot(qr, rot[:, k * w:(k + 1) * w].astype(BF16)))

    def softmax(s_parts):
        s = jnp.concatenate(s_parts, -1)
        s_new = (jnp.sum(ql.astype(F32) * cnr, -1, keepdims=True)
                 + jnp.sum(qr.astype(F32) * krnr, -1, keepdims=True))
        m = jnp.maximum(jnp.max(s, -1, keepdims=True), s_new)
        p = jnp.exp(s - m)
        p_new = jnp.exp(s_new - m)
        return p.astype(BF16), jnp.sum(p, -1, keepdims=True) + p_new, p_new

    def values(p, k):
        return _dot(p[:, k * w:(k + 1) * w], cbf[k * w:(k + 1) * w, :])

    def finish(v_parts, l, p_new):
        return (sum(v_parts) + p_new.astype(BF16).astype(F32) * cnr) / l

    return scores, softmax, values, finish


def _host_request(pt_ref, cache_c, cache_rt, kbuf, rbuf, cbf, sem, n_pages,
                  req, first, last, prime, sql_ref, sqr_ref, cn_ref, krn_ref, pieces):
    def page_copies(rq, sl):
        lat, rot = [], []
        for p in range(n_pages):
            page = pt_ref[rq * n_pages + p]
            toks = pl.ds(p * PAGE_SIZE, PAGE_SIZE)
            lat.append(pltpu.make_async_copy(cache_c.at[page], kbuf.at[sl, toks, :], sem.at[0, sl]))
            rot.append(pltpu.make_async_copy(cache_rt.at[page], rbuf.at[sl, :, toks], sem.at[1, sl]))
        return lat + rot

    sl = (req - first) % 2

    @pl.when(prime)
    def _():
        for cp in page_copies(req, sl):
            cp.start()

    for cp in page_copies(req, sl):
        cp.wait()
    nxt = jnp.minimum(req + 1, last)
    for cp in page_copies(nxt, 1 - sl):
        cp.start()
    cbf[...] = kbuf[sl].astype(BF16)
    fns = _paged_attention(sql_ref[req], sqr_ref[req], cn_ref[req], krn_ref[req], cbf, rbuf.at[sl], pieces)

    def drain():
        @pl.when(req == last)
        def _():
            for cp in page_copies(nxt, 1 - sl):
                cp.wait()

    return fns, drain


def _attn_prompt_kernel(tab_ref, pt_ref,
                        ql_ref, qr_ref, kt_ref, rt_ref, v_ref, x_ref, py_ref, wuv_ref, wout_ref,
                        sql_ref, sqr_ref, cn_ref, krn_ref, cache_c, cache_rt,
                        o_ref, so_ref,
                        m_ref, l_ref, acc_ref, kbuf, rbuf, cbf, sem,
                        *, tq, heads, n_pages, n_req, n_steps):
    step = pl.program_id(0)
    i = tab_ref[n_steps + step]
    j = tab_ref[2 * n_steps + step]
    req = tab_ref[3 * n_steps + step]
    tk = tq
    chains = [(hh, r) for hh in range(heads) for r in range(tq // ATTN_ROWS)]

    @pl.when(j == 0)
    def _():
        m_ref[...] = jnp.full_like(m_ref, -jnp.inf)
        l_ref[...] = jnp.zeros_like(l_ref)
        acc_ref[...] = jnp.zeros_like(acc_ref)

    def attend(diagonal, with_sample):
        if with_sample:
            pieces = len(chains) // 2
            (s_scores, s_softmax, s_values, s_finish), drain = _host_request(
                pt_ref, cache_c, cache_rt, kbuf, rbuf, cbf, sem, n_pages,
                req, 0, n_req - 1, req == 0, sql_ref, sqr_ref, cn_ref, krn_ref, pieces)
            s_parts, v_parts = [], []

        off = pl.multiple_of(j * tk, tk)
        kt = kt_ref[0, :, pl.ds(off, tk)]
        rt = rt_ref[0, :, pl.ds(off, tk)]
        v = v_ref[0, pl.ds(off, tk), :]

        def width(r):
            return (r + 1) * ATTN_ROWS if diagonal else tk

        def scores(ci):
            hh, r = chains[ci]
            rows = pl.ds(r * ATTN_ROWS, ATTN_ROWS)
            w = width(r)
            return _dot(ql_ref[0, hh, rows, :], kt[:, :w]) + _dot(qr_ref[0, hh, rows, :], rt[:, :w])

        pending = [scores(ci) for ci in range(ATTN_AHEAD)]
        for ci, (hh, r) in enumerate(chains):
            if ci + ATTN_AHEAD < len(chains):
                pending.append(scores(ci + ATTN_AHEAD))
            if with_sample and ci < pieces:
                s_parts.append(s_scores(ci))
            if with_sample and ci == pieces:
                p_sample, l_sample, p_new = s_softmax(s_parts)
            if with_sample and ci >= pieces:
                v_parts.append(s_values(p_sample, ci - pieces))
            s = pending[ci]
            if diagonal:
                qpos = r * ATTN_ROWS + lax.broadcasted_iota(jnp.int32, s.shape, 0)
                kpos = lax.broadcasted_iota(jnp.int32, s.shape, 1)
                s = jnp.where(kpos <= qpos, s, -jnp.inf)
            m_prev = m_ref[ci]
            m_new = jnp.maximum(m_prev, jnp.max(s, -1, keepdims=True))
            alpha = jnp.exp(m_prev - m_new)
            p = jnp.exp(s - _lane_tile(m_new, width(r) // LANES))
            l_ref[ci] = alpha * l_ref[ci] + jnp.sum(p, -1, keepdims=True)
            acc_ref[ci] = (_lane_tile(alpha, acc_ref.shape[-1] // LANES) * acc_ref[ci]
                           + _dot(p.astype(BF16), v[:width(r)]))
            m_ref[ci] = m_new
        if with_sample:
            so_ref[req] = s_finish(v_parts, l_sample, p_new)
            drain()

    for diagonal in (False, True):
        for with_sample in (False, True):
            on_diag = (j == i) if diagonal else (j < i)
            hosting = (req >= 0) if with_sample else (req < 0)
            pl.when(on_diag & hosting)(functools.partial(attend, diagonal, with_sample))

    @pl.when(j == i)
    def _():
        nr = tq // ATTN_ROWS
        rep = acc_ref.shape[-1] // LANES
        o_heads = [jnp.concatenate([acc_ref[hh * nr + r] / _lane_tile(l_ref[hh * nr + r], rep)
                                    for r in range(nr)], 0)
                   for hh in range(heads)]
        o_ref[0] = _mix_out(o_heads, py_ref[0], x_ref[0], wuv_ref, wout_ref)


def _mem_kv_kernel(mem_ref, g_ref, wk_ref, wv_ref, k_ref, v_ref, kbf_ref, vbf_ref):
    m = _rms(mem_ref[0], g_ref[...]).astype(BF16)
    k = _dot(m, wk_ref[...])
    v = _dot(m, wv_ref[...])
    k_ref[0] = k
    v_ref[0] = v
    kbf_ref[0] = k.astype(BF16)
    vbf_ref[0] = v.astype(BF16)


def _mem_attn_prompt_kernel(pt_ref, x_ref, g_ref, wq_ref, k_ref, v_ref, wo_ref,
                            sql_ref, sqr_ref, cn_ref, krn_ref, cache_c, cache_rt,
                            o_ref, so_ref, kbuf, rbuf, cbf, sem, *, heads, scale, n_pages, first):
    step = pl.program_id(0) * pl.num_programs(1) + pl.program_id(1)
    n_local = pl.num_programs(0) * pl.num_programs(1)
    pieces = 2 * heads
    (s_scores, s_softmax, s_values, s_finish), drain = _host_request(
        pt_ref, cache_c, cache_rt, kbuf, rbuf, cbf, sem, n_pages,
        first + step, first, first + n_local - 1, step == 0, sql_ref, sqr_ref, cn_ref, krn_ref, pieces)

    x = x_ref[0]
    q = (_dot(_rms(x, g_ref[...]).astype(BF16), wq_ref[...]) * scale).astype(BF16)
    s_parts = [s_scores(k) for k in range(heads)]
    hd = q.shape[-1] // heads
    outs = []
    for hh in range(heads):
        sl = slice(hh * hd, (hh + 1) * hd)
        s = _dot_nt(q[:, sl], k_ref[0, :, sl])
        s_parts.append(s_scores(heads + hh))
        p = jnp.exp(s - jnp.max(s, -1, keepdims=True))
        o = _dot(p.astype(BF16), v_ref[0, :, sl]) / jnp.sum(p, -1, keepdims=True)
        outs.append(o.astype(BF16))
    p_sample, l_sample, p_new = s_softmax(s_parts)
    v_parts = [s_values(p_sample, k) for k in range(heads)]
    o_ref[0] = x + _dot(jnp.concatenate(outs, -1), wo_ref[...])
    v_parts += [s_values(p_sample, heads + k) for k in range(heads)]
    so_ref[step] = s_finish(v_parts, l_sample, p_new)
    drain()


def _slab_col(r, heads, rr):
    return (r % heads) * (rr // heads) + r // heads


def _mem_query_slabs(x, g, wq_ref, q_ref, heads, scale):
    db = x.shape[0]
    rr = q_ref.shape[0] // db
    hq = _rms(x, g).astype(BF16)
    for r in range(rr):
        j = _slab_col(r, heads, rr)
        q_ref[pl.ds(r, db, stride=rr), :] = _dot(hq, wq_ref[:, j * LANES:(j + 1) * LANES]) * scale


def _mem_attend_request(q_ref, a_ref, k, v, request, heads, n_mem):
    rr = k.shape[0] // n_mem
    row = pl.multiple_of(request * rr, rr)
    qv = q_ref[pl.ds(row, rr), :]
    part = jnp.sum(k.reshape(n_mem, rr, LANES) * qv[None], -1, keepdims=True)
    s = part
    for d in range(1, rr // heads):
        s = s + pltpu.roll(part, d * heads, 1)
    p = jnp.exp(s - jnp.max(s, 0, keepdims=True))
    l = jnp.sum(p, 0)
    a_ref[pl.ds(row, rr), :] = jnp.sum(p * v.reshape(n_mem, rr, LANES), 0) / l


def _mem_output(x, a_ref, wo_ref, heads):
    db = x.shape[0]
    rr = a_ref.shape[0] // db
    acc = x
    for r in range(rr):
        j = _slab_col(r, heads, rr)
        a = a_ref[pl.ds(r, db, stride=rr), :].astype(BF16)
        acc = acc + _dot(a, wo_ref[j * LANES:(j + 1) * LANES, :])
    return acc


def _swiglu_final(x, g, wg_ref, wu_ref, wd_ref, gf):
    h = _rms(x, g).astype(BF16)
    gate = _dot(h, wg_ref[...])
    a = (gate * jax.nn.sigmoid(gate) * _dot(h, wu_ref[...])).astype(BF16)
    return _rms(x + _dot(a, wd_ref[...]), gf)


def _ffn_mem_kernel(x_ref, g_ref, wg_ref, wu_ref, wd_ref, gf_ref,
                    ol_ref, py_ref, xs_ref, wuv_ref, wout_ref, gx_ref, wq_ref, k_ref, v_ref, wo_ref,
                    y_ref, ys_ref, x1_ref, q_ref, a_ref, *, mla_heads, heads, scale, nb, n_mem):
    s_id = pl.program_id(0)

    @pl.when(s_id == 0)
    def _():
        x1 = _mix_out([ol_ref[hh] for hh in range(mla_heads)], py_ref[...], xs_ref[...], wuv_ref, wout_ref)
        x1_ref[...] = x1
        _mem_query_slabs(x1, gx_ref[...], wq_ref, q_ref, heads, scale)

    y_ref[...] = _swiglu_final(x_ref[...], g_ref[...], wg_ref, wu_ref, wd_ref, gf_ref[...])
    for bb in range(nb):
        _mem_attend_request(q_ref, a_ref, k_ref[bb], v_ref[bb], s_id * nb + bb, heads, n_mem)

    @pl.when(s_id == pl.num_programs(0) - 1)
    def _():
        x2 = _mem_output(x1_ref[...], a_ref, wo_ref, heads)
        ys_ref[...] = _swiglu_final(x2, g_ref[...], wg_ref, wu_ref, wd_ref, gf_ref[...])


def _rope_tables(pos, rope, heads):
    half = rope // 2
    inv = jnp.power(ROPE_THETA, -(jnp.arange(half, dtype=F32) / half))
    ang = pos.astype(F32)[:, None] * inv[None, :]
    cos, sin = jnp.cos(ang), jnp.sin(ang)
    return (jnp.tile(jnp.concatenate([cos, cos], -1), (1, heads)),
            jnp.tile(jnp.concatenate([-sin, sin], -1), (1, heads)))


def _swap_halves(w):
    half = w.shape[-1] // 2
    return jnp.concatenate([w[..., half:], w[..., :half]], -1)


def _row(v):
    return v.reshape(1, -1)


def _tile(n, pref):
    return pref if n % pref == 0 else n


def _attention_schedule(bsz, n_tiles, n_req):
    steps = [(b, i, j) for b in range(bsz) for i in range(n_tiles) for j in range(i + 1)]
    n_steps = len(steps)
    assert n_req <= n_steps, "at most one sample request per attention step"
    tab = np.zeros((4, n_steps), np.int32)
    for s, (b, i, j) in enumerate(steps):
        r = s * n_req // n_steps
        hosted = (s + 1) * n_req // n_steps > r
        tab[:, s] = (b, i, j, r if hosted else -1)
    return tab


def _attention_with_paged(qlat_p, qrope_p, ctbf, krtbf, cbf, x, py, wuv, wout,
                          page_table, qlat_s, qrope_s, c_new, kr_new, cache_c, cache_rt, tq, n_host):
    bsz, heads, seq, kv = qlat_p.shape
    rope = qrope_p.shape[-1]
    d = x.shape[-1]
    n_pages = page_table.shape[1]
    past = n_pages * PAGE_SIZE
    assert seq % tq == 0 and tq % ATTN_ROWS == 0
    n_chain = heads * (tq // ATTN_ROWS)
    tab = _attention_schedule(bsz, seq // tq, n_host)
    n_steps = tab.shape[1]
    col = lambda r: (lambda s, tab_ref, pt: tab_ref[r * n_steps + s])
    bb, ii = col(0), col(1)
    q_spec = lambda w: pl.BlockSpec((1, heads, tq, w), lambda *a: (bb(*a), 0, ii(*a), 0))
    batch_spec = lambda r, c: pl.BlockSpec((1, r, c), lambda *a: (bb(*a), 0, 0), pipeline_mode=pl.Buffered(1))
    row_spec = lambda w: pl.BlockSpec((1, tq, w), lambda *a: (bb(*a), ii(*a), 0))
    return pl.pallas_call(
        functools.partial(_attn_prompt_kernel, tq=tq, heads=heads, n_pages=n_pages, n_req=n_host,
                          n_steps=n_steps),
        grid_spec=pltpu.PrefetchScalarGridSpec(
            num_scalar_prefetch=2,
            grid=(n_steps,),
            in_specs=[q_spec(kv), q_spec(rope), batch_spec(kv, seq), batch_spec(rope, seq), batch_spec(seq, kv),
                      row_spec(d), row_spec(py.shape[-1]), _const_spec(wuv.shape), _const_spec(wout.shape),
                      _const_spec(qlat_s.shape), _const_spec(qrope_s.shape), _const_spec(c_new.shape),
                      _const_spec(kr_new.shape),
                      pl.BlockSpec(memory_space=pl.ANY), pl.BlockSpec(memory_space=pl.ANY)],
            out_specs=[row_spec(d), _whole_spec((n_host, heads, kv))],
            scratch_shapes=[pltpu.VMEM((n_chain, ATTN_ROWS, LANES), F32),
                            pltpu.VMEM((n_chain, ATTN_ROWS, LANES), F32),
                            pltpu.VMEM((n_chain, ATTN_ROWS, kv), F32),
                            pltpu.VMEM((2, past, kv), F32), pltpu.VMEM((2, rope, past), F32),
                            pltpu.VMEM((past, kv), BF16), pltpu.SemaphoreType.DMA((2, 2))],
        ),
        out_shape=[jax.ShapeDtypeStruct((bsz, seq, d), F32), jax.ShapeDtypeStruct((n_host, heads, kv), F32)],
        compiler_params=_params(1),
        name="attn_prompt_paged",
    )(jnp.asarray(tab.reshape(-1)), page_table.reshape(-1),
      qlat_p, qrope_p, ctbf, krtbf, cbf, x, py, wuv, wout,
      qlat_s, qrope_s, c_new, kr_new, cache_c, cache_rt)


def _ffn_with_sample_tail(x, g, wg, wu, wd, gf, olat, py, xs, wuv, wout, gx, wxq, mem_k, mem_v, wxo,
                          tm, heads, scale, n_mem):
    rows, d = x.shape
    db = xs.shape[0]
    n_steps = rows // tm
    assert rows % tm == 0 and db % n_steps == 0
    nb = db // n_steps
    slab = mem_k.shape[1] // n_mem
    x_spec = pl.BlockSpec((tm, d), lambda s: (s, 0))
    cache_spec = pl.BlockSpec((nb,) + mem_k.shape[1:], lambda s: (s, 0, 0))
    resident = (g, wg, wu, wd, gf, olat, py, xs, wuv, wout, gx, wxq)
    return pl.pallas_call(
        functools.partial(_ffn_mem_kernel, mla_heads=olat.shape[0], heads=heads, scale=scale, nb=nb,
                          n_mem=n_mem),
        grid=(n_steps,),
        in_specs=[x_spec] + [_const_spec(a.shape) for a in resident]
        + [cache_spec, cache_spec, _const_spec(wxo.shape)],
        out_specs=[x_spec, _whole_spec((db, d))],
        out_shape=[jax.ShapeDtypeStruct((rows, d), F32), jax.ShapeDtypeStruct((db, d), F32)],
        scratch_shapes=[pltpu.VMEM((db, d), F32), pltpu.VMEM((db * slab, LANES), F32),
                        pltpu.VMEM((db * slab, LANES), F32)],
        compiler_params=_params(1),
        name="ffn_mem",
    )(x, *resident, mem_k, mem_v, wxo)


def kernel(x_prompt, x_sample, mem_prompt, cache_kv_latent, cache_k_rope, cache_pool, cache_mem_k, cache_mem_v, page_table, norm_mix, w_in, q_norm, w_uq, kv_norm, w_uk, w_uv, w_pool, pool_scale, w_out, norm_x, mem_norm, w_xq, w_mk, w_mv, w_xo, norm_ffn, w_gate, w_up, w_down, norm_final):
    depth = w_in.shape[0]
    assert depth == 1, "single-layer trunk"
    bsz, seq, d = x_prompt.shape
    db, ts, _ = x_sample.shape
    assert ts == 1, "one new token per request"
    q_lora, kv_lora = q_norm.shape[1], kv_norm.shape[1]
    heads, nope = w_uk.shape[2], w_uk.shape[3]
    rope = cache_k_rope.shape[-1]
    pool_w = pool_scale.shape[1]
    v_head = w_uv.shape[3]
    n_mem, mem_heads, mem_hd = cache_mem_k.shape[2], cache_mem_k.shape[3], cache_mem_k.shape[4]
    n_pages = page_table.shape[1]
    assert cache_kv_latent.shape[2] == PAGE_SIZE
    past = n_pages * PAGE_SIZE
    mla_scale = float((nope + rope) ** -0.5)
    mem_scale = float(mem_hd ** -0.5)
    dims = (q_lora, kv_lora, pool_w, heads, nope, rope)
    l = 0

    i1, i2, i3 = q_lora, q_lora + kv_lora, q_lora + kv_lora + rope
    wi = w_in[l]
    w_kr = wi[:, i2:i3]
    win = jnp.concatenate([wi[:, :i2], wi[:, i3:], w_kr, _swap_halves(w_kr)], -1).astype(BF16)
    wq3 = w_uq[l].reshape(q_lora, heads, nope + rope)
    wq_rope = wq3[:, :, nope:]
    wuq = jnp.concatenate([wq3[:, :, :nope].reshape(q_lora, heads * nope),
                           wq_rope.reshape(q_lora, heads * rope),
                           _swap_halves(wq_rope).reshape(q_lora, heads * rope)], -1).astype(BF16)
    wuk = jnp.transpose(w_uk[l], (1, 2, 0)).astype(BF16)
    wuv = jnp.transpose(w_uv[l], (1, 0, 2)).astype(BF16)
    wpool = w_pool[l].astype(BF16)
    wout = w_out[l].astype(BF16)
    wxq, wxo = w_xq[l].astype(BF16), w_xo[l].astype(BF16)
    wmk, wmv = w_mk[l].astype(BF16), w_mv[l].astype(BF16)
    wg, wu, wd = w_gate[l].astype(BF16), w_up[l].astype(BF16), w_down[l].astype(BF16)
    gmix, gq, gkv = _row(norm_mix[l]), _row(q_norm[l]), _row(kv_norm[l])
    gx, gmem, gffn, gfin = _row(norm_x[l]), _row(mem_norm[l]), _row(norm_ffn[l]), _row(norm_final)
    pscale = _row(pool_scale[l])
    front_consts = (gmix, win, gq, wuq, gkv, wuk)
    hr = heads * rope

    tm = _tile(seq, 512)
    cos_p, sin_p = _rope_tables(jnp.arange(seq), rope, heads)
    row_spec = lambda w: pl.BlockSpec((1, tm, w), lambda b, t: (b, t, 0))
    head_spec = lambda w: pl.BlockSpec((1, heads, tm, w), lambda b, t: (b, 0, t, 0))
    tab_spec = pl.BlockSpec((tm, hr), lambda b, t: (t, 0))
    col_spec = lambda w: pl.BlockSpec((1, w, tm), lambda b, t: (b, 0, t))
    qlat_p, qrope_p, c_p, krt_p, cbf_p, ctbf_p, krtbf_p, py_p, pst_p = pl.pallas_call(
        functools.partial(_front_prompt_kernel, scale=mla_scale, dims=dims, tm=tm),
        grid=(bsz, seq // tm),
        in_specs=[row_spec(d)] + [_const_spec(a.shape) for a in front_consts]
        + [tab_spec, tab_spec, _const_spec(wpool.shape), _const_spec(pscale.shape)],
        out_specs=[head_spec(kv_lora), head_spec(rope), row_spec(kv_lora), col_spec(rope),
                   row_spec(kv_lora), col_spec(kv_lora), col_spec(rope), row_spec(pool_w),
                   pl.BlockSpec((1, POOL_HALO, pool_w), lambda b, t: (b, 0, 0))],
        out_shape=[jax.ShapeDtypeStruct((bsz, heads, seq, kv_lora), BF16),
                   jax.ShapeDtypeStruct((bsz, heads, seq, rope), BF16),
                   jax.ShapeDtypeStruct((bsz, seq, kv_lora), F32),
                   jax.ShapeDtypeStruct((bsz, rope, seq), F32),
                   jax.ShapeDtypeStruct((bsz, seq, kv_lora), BF16),
                   jax.ShapeDtypeStruct((bsz, kv_lora, seq), BF16),
                   jax.ShapeDtypeStruct((bsz, rope, seq), BF16),
                   jax.ShapeDtypeStruct((bsz, seq, pool_w), BF16),
                   jax.ShapeDtypeStruct((bsz, POOL_HALO, pool_w), F32)],
        scratch_shapes=[pltpu.VMEM((POOL_HALO, pool_w), F32)],
        compiler_params=_params(2),
        name="front_prompt",
    )(x_prompt, *front_consts, cos_p, sin_p, wpool, pscale)

    xs = x_sample.reshape(db, d)
    cos_s, sin_s = _rope_tables(jnp.full((db,), past), rope, heads)
    prev_s = jnp.transpose(cache_pool[l], (1, 0, 2))
    sample_in = (xs, *front_consts, cos_s, sin_s, wpool, pscale, prev_s)
    qlat_s, qrope_s, c_s, kr_s, py_s, up_s = pl.pallas_call(
        functools.partial(_front_sample_kernel, scale=mla_scale, dims=dims, past=past),
        grid=(1,),
        in_specs=[_const_spec(a.shape) for a in sample_in],
        out_specs=[_whole_spec(s) for s in ((heads, db, kv_lora), (heads, db, rope), (db, kv_lora),
                                            (db, rope), (db, pool_w), (db, pool_w))],
        out_shape=[jax.ShapeDtypeStruct((heads, db, kv_lora), BF16),
                   jax.ShapeDtypeStruct((heads, db, rope), BF16),
                   jax.ShapeDtypeStruct((db, kv_lora), F32),
                   jax.ShapeDtypeStruct((db, rope), F32),
                   jax.ShapeDtypeStruct((db, pool_w), BF16),
                   jax.ShapeDtypeStruct((db, pool_w), F32)],
        compiler_params=_params(1),
        name="front_sample",
    )(*sample_in)

    tmem = _tile(seq, 512)
    n_host_mem = min(db, bsz * (seq // tmem))
    n_host_attn = db - n_host_mem
    sample_ops = (jnp.swapaxes(qlat_s, 0, 1), jnp.swapaxes(qrope_s, 0, 1),
                  c_s.reshape(db, 1, kv_lora), kr_s.reshape(db, 1, rope))
    cache_c = cache_kv_latent[l]
    cache_rt = jnp.swapaxes(cache_k_rope[l], 1, 2)
    x1_p, olat_attn = _attention_with_paged(
        qlat_p, qrope_p, ctbf_p, krtbf_p, cbf_p, x_prompt, py_p, wuv, wout,
        page_table, *sample_ops, cache_c, cache_rt, tq=_tile(seq, 512), n_host=n_host_attn)

    mem_spec = pl.BlockSpec((1, n_mem, d), lambda b: (b, 0, 0))
    mk_p, mv_p, mkbf_p, mvbf_p = pl.pallas_call(
        _mem_kv_kernel,
        grid=(bsz,),
        in_specs=[mem_spec, _const_spec(gmem.shape), _const_spec(wmk.shape), _const_spec(wmv.shape)],
        out_specs=[mem_spec] * 4,
        out_shape=[jax.ShapeDtypeStruct((bsz, n_mem, d), F32)] * 2
        + [jax.ShapeDtypeStruct((bsz, n_mem, d), BF16)] * 2,
        compiler_params=_params(1),
        name="mem_kv",
    )(mem_prompt, gmem, wmk, wmv)

    assert n_host_attn > 0 and n_host_mem == bsz * (seq // tmem), "every memory-attention step hosts a request"
    xm_spec = pl.BlockSpec((1, tmem, d), lambda b, i, pt: (b, i, 0))
    memkv_spec = pl.BlockSpec((1, n_mem, d), lambda b, i, pt: (b, 0, 0))
    x2_p, olat_mem = pl.pallas_call(
        functools.partial(_mem_attn_prompt_kernel, heads=mem_heads, scale=mem_scale, n_pages=n_pages,
                          first=n_host_attn),
        grid_spec=pltpu.PrefetchScalarGridSpec(
            num_scalar_prefetch=1,
            grid=(bsz, seq // tmem),
            in_specs=[xm_spec, _const_spec(gx.shape), _const_spec(wxq.shape), memkv_spec, memkv_spec,
                      _const_spec(wxo.shape)] + [_const_spec(a.shape) for a in sample_ops]
            + [pl.BlockSpec(memory_space=pl.ANY), pl.BlockSpec(memory_space=pl.ANY)],
            out_specs=[xm_spec, _whole_spec((n_host_mem, heads, kv_lora))],
            scratch_shapes=[pltpu.VMEM((2, past, kv_lora), F32), pltpu.VMEM((2, rope, past), F32),
                            pltpu.VMEM((past, kv_lora), BF16), pltpu.SemaphoreType.DMA((2, 2))],
        ),
        out_shape=[jax.ShapeDtypeStruct((bsz, seq, d), F32),
                   jax.ShapeDtypeStruct((n_host_mem, heads, kv_lora), F32)],
        compiler_params=_params(2),
        name="mem_attn_prompt",
    )(page_table.reshape(-1), x1_p, gx, wxq, mkbf_p, mvbf_p, wxo, *sample_ops, cache_c, cache_rt)
    olat_s = jnp.concatenate([olat_attn, olat_mem], 0)

    assert mem_hd % LANES == 0
    slab = mem_heads * (mem_hd // LANES)

    def slab_view(a):
        a = a.reshape(db, n_mem, mem_heads, mem_hd // LANES, LANES)
        return jnp.transpose(a, (0, 1, 3, 2, 4)).reshape(db, n_mem * slab, LANES)

    y_prompt, y_sample = _ffn_with_sample_tail(
        x2_p.reshape(bsz * seq, d), gffn, wg, wu, wd, gfin,
        jnp.swapaxes(olat_s, 0, 1), py_s, xs, wuv, wout,
        gx, wxq, slab_view(cache_mem_k[l]), slab_view(cache_mem_v[l]), wxo,
        tm=_tile(bsz * seq, 256), heads=mem_heads, scale=mem_scale, n_mem=n_mem)
    y_prompt = y_prompt.reshape(bsz, seq, d)
    y_sample = y_sample.reshape(db, ts, d)

    mem_shape = (depth, bsz, n_mem, mem_heads, mem_hd)
    return (y_prompt, y_sample,
            c_p[None], jnp.swapaxes(krt_p, 1, 2)[None], pst_p[None, :, POOL_HALO - POOL_STATE:],
            mk_p.reshape(mem_shape), mv_p.reshape(mem_shape),
            c_s.reshape(depth, db, ts, kv_lora), kr_s.reshape(depth, db, ts, rope),
            jnp.concatenate([cache_pool[l][:, 1:], up_s[:, None]], 1)[None])
```

```python
import functools

import jax
import jax.numpy as jnp
import numpy as np
from jax import lax
from jax.experimental import pallas as pl
from jax.experimental.pallas import tpu as pltpu

F32 = jnp.float32
BF16 = jnp.bfloat16

EPS = 1e-6
ROPE_THETA = 10000.0
PAGE_SIZE = 128
POOL_WINDOWS = (2, 4, 8, 16)
POOL_STATE = max(POOL_WINDOWS) - 1
POOL_HALO = 16
LANES = 128
VMEM_LIMIT = 56 * 1024 * 1024
ATTN_ROWS = 256
ATTN_AHEAD = 3


def _rms(x, g):
    return x * lax.rsqrt(jnp.mean(x * x, -1, keepdims=True) + EPS) * g


def _dot(a, b):
    return jnp.dot(a, b, preferred_element_type=F32)


def _dot_nt(a, b):
    return lax.dot_general(a, b, (((1,), (1,)), ((), ())), preferred_element_type=F32)


def _lane_tile(x, n):
    return jnp.concatenate([x] * n, -1)


def _const_spec(shape):
    nd = len(shape)
    return pl.BlockSpec(shape, lambda *_: (0,) * nd, pipeline_mode=pl.Buffered(1))


def _whole_spec(shape):
    nd = len(shape)
    return pl.BlockSpec(shape, lambda *_: (0,) * nd)


def _params(n_axes):
    return pltpu.CompilerParams(dimension_semantics=("arbitrary",) * n_axes,
                                vmem_limit_bytes=VMEM_LIMIT)


def _mla_inputs(x, gmix, win, gq, wuq, gkv, wuk, cos, sin, scale, dims):
    q_lora, kv_lora, pool_w, heads, nope, rope = dims
    h = _rms(x, gmix).astype(BF16)
    u = _dot(h, win)
    i1, i2, i3 = q_lora, q_lora + kv_lora, q_lora + kv_lora + pool_w
    cq, ckv, up, krs = u[:, :i1], u[:, i1:i2], u[:, i2:i3], u[:, i3:]
    q = _dot(_rms(cq, gq).astype(BF16), wuq)
    n0 = heads * nope
    n1 = n0 + heads * rope
    q_rope = (q[:, n0:n1] * cos + q[:, n1:] * sin) * scale
    q_lat = [(_dot(q[:, hh * nope:(hh + 1) * nope].astype(BF16), wuk[hh]) * scale).astype(BF16)
             for hh in range(heads)]
    q_rope = [q_rope[:, hh * rope:(hh + 1) * rope].astype(BF16) for hh in range(heads)]
    c = _rms(ckv, gkv)
    k_rope = krs * cos[:, :2 * rope] + pltpu.roll(krs, rope, 1) * sin[:, :2 * rope]
    return q_lat, q_rope, c, k_rope, up


def _pool_project(sums, up, cnt, wpool, pscale):
    cg = wpool.shape[-1]
    ys = []
    for g in range(len(POOL_WINDOWS)):
        m = sums[g] / cnt[g] - up[:, g * cg:(g + 1) * cg]
        ys.append(_dot(m.astype(BF16), wpool[g]))
    return jnp.concatenate(ys, -1) * pscale


def _front_prompt_kernel(x_ref, gmix_ref, win_ref, gq_ref, wuq_ref, gkv_ref, wuk_ref, cos_ref, sin_ref,
                         wpool_ref, pscale_ref,
                         qlat_ref, qrope_ref, c_ref, krt_ref, cbf_ref, ctbf_ref, krtbf_ref, py_ref, pst_ref,
                         prev_ref, *, scale, dims, tm):
    t = pl.program_id(1)
    heads, rope = dims[3], dims[5]
    q_lat, q_rope, c, k_rope, up = _mla_inputs(
        x_ref[0], gmix_ref[...], win_ref[...], gq_ref[...], wuq_ref[...], gkv_ref[...], wuk_ref,
        cos_ref[...], sin_ref[...], scale, dims)
    for hh in range(heads):
        qlat_ref[0, hh] = q_lat[hh]
        qrope_ref[0, hh] = q_rope[hh]
    c_ref[0] = c
    cbf_ref[0] = c.astype(BF16)
    ctbf_ref[0] = c.T.astype(BF16)
    krt = k_rope.T[:rope]
    krt_ref[0] = krt
    krtbf_ref[0] = krt.astype(BF16)

    @pl.when(t == 0)
    def _():
        prev_ref[...] = jnp.zeros_like(prev_ref)

    cg = wpool_ref.shape[-1]
    e = jnp.concatenate([prev_ref[...], up], 0)
    tail = up[tm - POOL_HALO:, :]
    prev_ref[...] = tail
    pst_ref[0] = tail
    sums = []
    s = e
    for g, w in enumerate(POOL_WINDOWS):
        s = s[:, (cg if g else 0):]
        s = s + pltpu.roll(s, w // 2, 0)
        sums.append(s[POOL_HALO:, :cg])
    pos1 = t * tm + lax.broadcasted_iota(jnp.int32, (tm, cg), 0) + 1
    cnt = [jnp.minimum(w, pos1).astype(F32) for w in POOL_WINDOWS]
    py_ref[0] = _pool_project(sums, up, cnt, wpool_ref, pscale_ref[...]).astype(BF16)


def _front_sample_kernel(x_ref, gmix_ref, win_ref, gq_ref, wuq_ref, gkv_ref, wuk_ref, cos_ref, sin_ref,
                         wpool_ref, pscale_ref, prev_ref,
                         qlat_ref, qrope_ref, c_ref, kr_ref, py_ref, up_ref, *, scale, dims, past):
    heads, rope = dims[3], dims[5]
    q_lat, q_rope, c, k_rope, up = _mla_inputs(
        x_ref[...], gmix_ref[...], win_ref[...], gq_ref[...], wuq_ref[...], gkv_ref[...], wuk_ref,
        cos_ref[...], sin_ref[...], scale, dims)
    for hh in range(heads):
        qlat_ref[hh] = q_lat[hh]
        qrope_ref[hh] = q_rope[hh]
    c_ref[...] = c
    kr_ref[...] = k_rope[:, :rope]
    up_ref[...] = up

    cg = wpool_ref.shape[-1]
    sums, cnt = [], []
    for g, w in enumerate(POOL_WINDOWS):
        s = up[:, g * cg:(g + 1) * cg]
        for j in range(1, w):
            s = s + prev_ref[POOL_STATE - j][:, g * cg:(g + 1) * cg]
        sums.append(s)
        cnt.append(float(min(w, past + 1)))
    py_ref[...] = _pool_project(sums, up, cnt, wpool_ref, pscale_ref[...]).astype(BF16)


def _mix_out(o_lat_heads, py, x, wuv_ref, wout_ref):
    o_mla = jnp.concatenate(
        [_dot(o.astype(BF16), wuv_ref[hh]) for hh, o in enumerate(o_lat_heads)], -1).astype(BF16)
    wm = o_mla.shape[-1]
    return x + _dot(o_mla, wout_ref[:wm, :]) + _dot(py, wout_ref[wm:, :])


def _paged_attention(ql, qr, cn, krn, cbf, rot, pieces):
    cnr = cn.astype(BF16).astype(F32)
    krnr = krn.astype(BF16).astype(F32)
    w = cbf.shape[0] // pieces

    def scores(k):
        return (_dot_nt(ql, cbf[k * w:(k + 1) * w, :])
                + _dot(qr, rot[:, k * w:(k + 1) * w].astype(BF16)))

    def softmax(s_parts):
        s = jnp.concatenate(s_parts, -1)
        s_new = (jnp.sum(ql.astype(F32) * cnr, -1, keepdims=True)
                 + jnp.sum(qr.astype(F32) * krnr, -1, keepdims=True))
        m = jnp.maximum(jnp.max(s, -1, keepdims=True), s_new)
        p = jnp.exp(s - m)
        p_new = jnp.exp(s_new - m)
        return p.astype(BF16), jnp.sum(p, -1, keepdims=True) + p_new, p_new

    def values(p, k):
        return _dot(p[:, k * w:(k + 1) * w], cbf[k * w:(k + 1) * w, :])

    def finish(v_parts, l, p_new):
        return (sum(v_parts) + p_new.astype(BF16).astype(F32) * cnr) / l

    return scores, softmax, values, finish


def _host_request(pt_ref, cache_c, cache_rt, kbuf, rbuf, cbf, sem, n_pages,
                  req, first, last, prime, sql_ref, sqr_ref, cn_ref, krn_ref, pieces):
    def page_copies(rq, sl):
        lat, rot = [], []
        for p in range(n_pages):
            page = pt_ref[rq * n_pages + p]
            toks = pl.ds(p * PAGE_SIZE, PAGE_SIZE)
            lat.append(pltpu.make_async_copy(cache_c.at[page], kbuf.at[sl, toks, :], sem.at[0, sl]))
            rot.append(pltpu.make_async_copy(cache_rt.at[page], rbuf.at[sl, :, toks], sem.at[1, sl]))
        return lat + rot

    sl = (req - first) % 2

    @pl.when(prime)
    def _():
        for cp in page_copies(req, sl):
            cp.start()

    for cp in page_copies(req, sl):
        cp.wait()
    nxt = jnp.minimum(req + 1, last)
    for cp in page_copies(nxt, 1 - sl):
        cp.start()
    cbf[...] = kbuf[sl].astype(BF16)
    fns = _paged_attention(sql_ref[req], sqr_ref[req], cn_ref[req], krn_ref[req], cbf, rbuf.at[sl], pieces)

    def drain():
        @pl.when(req == last)
        def _():
            for cp in page_copies(nxt, 1 - sl):
                cp.wait()

    return fns, drain


def _attn_prompt_kernel(tab_ref, pt_ref,
                        ql_ref, qr_ref, kt_ref, rt_ref, v_ref, x_ref, py_ref, wuv_ref, wout_ref,
                        sql_ref, sqr_ref, cn_ref, krn_ref, cache_c, cache_rt,
                        o_ref, so_ref,
                        m_ref, l_ref, acc_ref, kbuf, rbuf, cbf, sem,
                        *, tq, heads, n_pages, n_req, n_steps):
    step = pl.program_id(0)
    i = tab_ref[n_steps + step]
    j = tab_ref[2 * n_steps + step]
    req = tab_ref[3 * n_steps + step]
    tk = tq
    chains = [(hh, r) for hh in range(heads) for r in range(tq // ATTN_ROWS)]

    @pl.when(j == 0)
    def _():
        m_ref[...] = jnp.full_like(m_ref, -jnp.inf)
        l_ref[...] = jnp.zeros_like(l_ref)
        acc_ref[...] = jnp.zeros_like(acc_ref)

    def attend(diagonal, with_sample):
        if with_sample:
            pieces = len(chains) // 2
            (s_scores, s_softmax, s_values, s_finish), drain = _host_request(
                pt_ref, cache_c, cache_rt, kbuf, rbuf, cbf, sem, n_pages,
                req, 0, n_req - 1, req == 0, sql_ref, sqr_ref, cn_ref, krn_ref, pieces)
            s_parts, v_parts = [], []

        off = pl.multiple_of(j * tk, tk)
        kt = kt_ref[0, :, pl.ds(off, tk)]
        rt = rt_ref[0, :, pl.ds(off, tk)]
        v = v_ref[0, pl.ds(off, tk), :]

        def width(r):
            return (r + 1) * ATTN_ROWS if diagonal else tk

        def scores(ci):
            hh, r = chains[ci]
            rows = pl.ds(r * ATTN_ROWS, ATTN_ROWS)
            w = width(r)
            return _dot(ql_ref[0, hh, rows, :], kt[:, :w]) + _dot(qr_ref[0, hh, rows, :], rt[:, :w])

        pending = [scores(ci) for ci in range(ATTN_AHEAD)]
        for ci, (hh, r) in enumerate(chains):
            if ci + ATTN_AHEAD < len(chains):
                pending.append(scores(ci + ATTN_AHEAD))
            if with_sample and ci < pieces:
                s_parts.append(s_scores(ci))
            if with_sample and ci == pieces:
                p_sample, l_sample, p_new = s_softmax(s_parts)
            if with_sample and ci >= pieces:
                v_parts.append(s_values(p_sample, ci - pieces))
            s = pending[ci]
            if diagonal:
                qpos = r * ATTN_ROWS + lax.broadcasted_iota(jnp.int32, s.shape, 0)
                kpos = lax.broadcasted_iota(jnp.int32, s.shape, 1)
                s = jnp.where(kpos <= qpos, s, -jnp.inf)
            m_prev = m_ref[ci]
            m_new = jnp.maximum(m_prev, jnp.max(s, -1, keepdims=True))
            alpha = jnp.exp(m_prev - m_new)
            p = jnp.exp(s - _lane_tile(m_new, width(r) // LANES))
            l_ref[ci] = alpha * l_ref[ci] + jnp.sum(p, -1, keepdims=True)
            acc_ref[ci] = (_lane_tile(alpha, acc_ref.shape[-1] // LANES) * acc_ref[ci]
                           + _dot(p.astype(BF16), v[:width(r)]))
            m_ref[ci] = m_new
        if with_sample:
            so_ref[req] = s_finish(v_parts, l_sample, p_new)
            drain()

    for diagonal in (False, True):
        for with_sample in (False, True):
            on_diag = (j == i) if diagonal else (j < i)
            hosting = (req >= 0) if with_sample else (req < 0)
            pl.when(on_diag & hosting)(functools.partial(attend, diagonal, with_sample))

    @pl.when(j == i)
    def _():
        nr = tq // ATTN_ROWS
        rep = acc_ref.shape[-1] // LANES
        o_heads = [jnp.concatenate([acc_ref[hh * nr + r] / _lane_tile(l_ref[hh * nr + r], rep)
                                    for r in range(nr)], 0)
                   for hh in range(heads)]
        o_ref[0] = _mix_out(o_heads, py_ref[0], x_ref[0], wuv_ref, wout_ref)


def _mem_attn_prompt_kernel(x_ref, g_ref, wq_ref, mem_ref, gm_ref, wk_ref, wv_ref, wo_ref,
                            o_ref, mk_ref, mv_ref, kbf, vbf, *, heads, scale):
    @pl.when(pl.program_id(1) == 0)
    def _():
        m = _rms(mem_ref[0], gm_ref[...]).astype(BF16)
        n_mem = m.shape[0]
        rr = mk_ref.shape[1] // n_mem
        for w_ref, out_ref, bf in ((wk_ref, mk_ref, kbf), (wv_ref, mv_ref, vbf)):
            kv = _dot(m, w_ref[...])
            bf[...] = kv.astype(BF16)
            for r in range(rr):
                j = _slab_col(r, heads, rr)
                out_ref[0, pl.ds(r, n_mem, stride=rr), :] = kv[:, j * LANES:(j + 1) * LANES]

    x = x_ref[0]
    q = (_dot(_rms(x, g_ref[...]).astype(BF16), wq_ref[...]) * scale).astype(BF16)
    hd = q.shape[-1] // heads
    outs = []
    for hh in range(heads):
        sl = slice(hh * hd, (hh + 1) * hd)
        s = _dot_nt(q[:, sl], kbf[:, sl])
        p = jnp.exp(s - jnp.max(s, -1, keepdims=True))
        o = _dot(p.astype(BF16), vbf[:, sl]) / jnp.sum(p, -1, keepdims=True)
        outs.append(o.astype(BF16))
    o_ref[0] = x + _dot(jnp.concatenate(outs, -1), wo_ref[...])


def _slab_col(r, heads, rr):
    return (r % heads) * (rr // heads) + r // heads


def _mem_query_slabs(x, g, wq_ref, q_ref, heads, scale):
    db = x.shape[0]
    rr = q_ref.shape[0] // db
    hq = _rms(x, g).astype(BF16)
    for r in range(rr):
        j = _slab_col(r, heads, rr)
        q_ref[pl.ds(r, db, stride=rr), :] = _dot(hq, wq_ref[:, j * LANES:(j + 1) * LANES]) * scale


def _mem_attend_request(q_ref, a_ref, k, v, request, heads, n_mem):
    rr = k.shape[0] // n_mem
    row = pl.multiple_of(request * rr, rr)
    qv = q_ref[pl.ds(row, rr), :]
    part = jnp.sum(k.reshape(n_mem, rr, LANES) * qv[None], -1, keepdims=True)
    s = part
    for d in range(1, rr // heads):
        s = s + pltpu.roll(part, d * heads, 1)
    p = jnp.exp(s - jnp.max(s, 0, keepdims=True))
    l = jnp.sum(p, 0)
    a_ref[pl.ds(row, rr), :] = jnp.sum(p * v.reshape(n_mem, rr, LANES), 0) / l


def _mem_output(x, a_ref, wo_ref, heads):
    db = x.shape[0]
    rr = a_ref.shape[0] // db
    acc = x
    for r in range(rr):
        j = _slab_col(r, heads, rr)
        a = a_ref[pl.ds(r, db, stride=rr), :].astype(BF16)
        acc = acc + _dot(a, wo_ref[j * LANES:(j + 1) * LANES, :])
    return acc


def _swiglu_final(x, g, wg_ref, wu_ref, wd_ref, gf):
    h = _rms(x, g).astype(BF16)
    gate = _dot(h, wg_ref[...])
    a = (gate * jax.nn.sigmoid(gate) * _dot(h, wu_ref[...])).astype(BF16)
    return _rms(x + _dot(a, wd_ref[...]), gf)


def _ffn_mem_kernel(x_ref, g_ref, wg_ref, wu_ref, wd_ref, gf_ref,
                    ol_ref, py_ref, xs_ref, wuv_ref, wout_ref, gx_ref, wq_ref, k_ref, v_ref, wo_ref,
                    y_ref, ys_ref, x1_ref, q_ref, a_ref, *, mla_heads, heads, scale, nb, n_mem):
    s_id = pl.program_id(0)

    @pl.when(s_id == 0)
    def _():
        x1 = _mix_out([ol_ref[hh] for hh in range(mla_heads)], py_ref[...], xs_ref[...], wuv_ref, wout_ref)
        x1_ref[...] = x1
        _mem_query_slabs(x1, gx_ref[...], wq_ref, q_ref, heads, scale)

    y_ref[...] = _swiglu_final(x_ref[...], g_ref[...], wg_ref, wu_ref, wd_ref, gf_ref[...])
    for bb in range(nb):
        _mem_attend_request(q_ref, a_ref, k_ref[bb], v_ref[bb], s_id * nb + bb, heads, n_mem)

    @pl.when(s_id == pl.num_programs(0) - 1)
    def _():
        x2 = _mem_output(x1_ref[...], a_ref, wo_ref, heads)
        ys_ref[...] = _swiglu_final(x2, g_ref[...], wg_ref, wu_ref, wd_ref, gf_ref[...])


def _rope_tables(pos, rope, heads):
    half = rope // 2
    inv = jnp.power(ROPE_THETA, -(jnp.arange(half, dtype=F32) / half))
    ang = pos.astype(F32)[:, None] * inv[None, :]
    cos, sin = jnp.cos(ang), jnp.sin(ang)
    return (jnp.tile(jnp.concatenate([cos, cos], -1), (1, heads)),
            jnp.tile(jnp.concatenate([-sin, sin], -1), (1, heads)))


def _swap_halves(w):
    half = w.shape[-1] // 2
    return jnp.concatenate([w[..., half:], w[..., :half]], -1)


def _row(v):
    return v.reshape(1, -1)


def _tile(n, pref):
    return pref if n % pref == 0 else n


def _attention_schedule(bsz, n_tiles, n_req):
    steps = [(b, i, j) for b in range(bsz) for i in range(n_tiles) for j in range(i + 1)]
    n_steps = len(steps)
    assert n_req <= n_steps, "at most one sample request per attention step"
    tab = np.zeros((4, n_steps), np.int32)
    for s, (b, i, j) in enumerate(steps):
        r = s * n_req // n_steps
        hosted = (s + 1) * n_req // n_steps > r
        tab[:, s] = (b, i, j, r if hosted else -1)
    return tab


def _attention_with_paged(qlat_p, qrope_p, ctbf, krtbf, cbf, x, py, wuv, wout,
                          page_table, qlat_s, qrope_s, c_new, kr_new, cache_c, cache_rt, tq, n_host):
    bsz, heads, seq, kv = qlat_p.shape
    rope = qrope_p.shape[-1]
    d = x.shape[-1]
    n_pages = page_table.shape[1]
    past = n_pages * PAGE_SIZE
    assert seq % tq == 0 and tq % ATTN_ROWS == 0
    n_chain = heads * (tq // ATTN_ROWS)
    tab = _attention_schedule(bsz, seq // tq, n_host)
    n_steps = tab.shape[1]
    col = lambda r: (lambda s, tab_ref, pt: tab_ref[r * n_steps + s])
    bb, ii = col(0), col(1)
    q_spec = lambda w: pl.BlockSpec((1, heads, tq, w), lambda *a: (bb(*a), 0, ii(*a), 0))
    batch_spec = lambda r, c: pl.BlockSpec((1, r, c), lambda *a: (bb(*a), 0, 0), pipeline_mode=pl.Buffered(1))
    row_spec = lambda w: pl.BlockSpec((1, tq, w), lambda *a: (bb(*a), ii(*a), 0))
    return pl.pallas_call(
        functools.partial(_attn_prompt_kernel, tq=tq, heads=heads, n_pages=n_pages, n_req=n_host,
                          n_steps=n_steps),
        grid_spec=pltpu.PrefetchScalarGridSpec(
            num_scalar_prefetch=2,
            grid=(n_steps,),
            in_specs=[q_spec(kv), q_spec(rope), batch_spec(kv, seq), batch_spec(rope, seq), batch_spec(seq, kv),
                      row_spec(d), row_spec(py.shape[-1]), _const_spec(wuv.shape), _const_spec(wout.shape),
                      _const_spec(qlat_s.shape), _const_spec(qrope_s.shape), _const_spec(c_new.shape),
                      _const_spec(kr_new.shape),
                      pl.BlockSpec(memory_space=pl.ANY), pl.BlockSpec(memory_space=pl.ANY)],
            out_specs=[row_spec(d), _whole_spec((n_host, heads, kv))],
            scratch_shapes=[pltpu.VMEM((n_chain, ATTN_ROWS, LANES), F32),
                            pltpu.VMEM((n_chain, ATTN_ROWS, LANES), F32),
                            pltpu.VMEM((n_chain, ATTN_ROWS, kv), F32),
                            pltpu.VMEM((2, past, kv), F32), pltpu.VMEM((2, rope, past), F32),
                            pltpu.VMEM((past, kv), BF16), pltpu.SemaphoreType.DMA((2, 2))],
        ),
        out_shape=[jax.ShapeDtypeStruct((bsz, seq, d), F32), jax.ShapeDtypeStruct((n_host, heads, kv), F32)],
        compiler_params=_params(1),
        name="attn_prompt_paged",
    )(jnp.asarray(tab.reshape(-1)), page_table.reshape(-1),
      qlat_p, qrope_p, ctbf, krtbf, cbf, x, py, wuv, wout,
      qlat_s, qrope_s, c_new, kr_new, cache_c, cache_rt)


def _ffn_with_sample_tail(x, g, wg, wu, wd, gf, olat, py, xs, wuv, wout, gx, wxq, mem_k, mem_v, wxo,
                          tm, heads, scale, n_mem):
    rows, d = x.shape
    db = xs.shape[0]
    n_steps = rows // tm
    assert rows % tm == 0 and db % n_steps == 0
    nb = db // n_steps
    slab = mem_k.shape[1] // n_mem
    x_spec = pl.BlockSpec((tm, d), lambda s: (s, 0))
    cache_spec = pl.BlockSpec((nb,) + mem_k.shape[1:], lambda s: (s, 0, 0))
    resident = (g, wg, wu, wd, gf, olat, py, xs, wuv, wout, gx, wxq)
    return pl.pallas_call(
        functools.partial(_ffn_mem_kernel, mla_heads=olat.shape[0], heads=heads, scale=scale, nb=nb,
                          n_mem=n_mem),
        grid=(n_steps,),
        in_specs=[x_spec] + [_const_spec(a.shape) for a in resident]
        + [cache_spec, cache_spec, _const_spec(wxo.shape)],
        out_specs=[x_spec, _whole_spec((db, d))],
        out_shape=[jax.ShapeDtypeStruct((rows, d), F32), jax.ShapeDtypeStruct((db, d), F32)],
        scratch_shapes=[pltpu.VMEM((db, d), F32), pltpu.VMEM((db * slab, LANES), F32),
                        pltpu.VMEM((db * slab, LANES), F32)],
        compiler_params=_params(1),
        name="ffn_mem",
    )(x, *resident, mem_k, mem_v, wxo)


def kernel(x_prompt, x_sample, mem_prompt, cache_kv_latent, cache_k_rope, cache_pool, cache_mem_k, cache_mem_v, page_table, norm_mix, w_in, q_norm, w_uq, kv_norm, w_uk, w_uv, w_pool, pool_scale, w_out, norm_x, mem_norm, w_xq, w_mk, w_mv, w_xo, norm_ffn, w_gate, w_up, w_down, norm_final):
    depth = w_in.shape[0]
    assert depth == 1, "single-layer trunk"
    bsz, seq, d = x_prompt.shape
    db, ts, _ = x_sample.shape
    assert ts == 1, "one new token per request"
    q_lora, kv_lora = q_norm.shape[1], kv_norm.shape[1]
    heads, nope = w_uk.shape[2], w_uk.shape[3]
    rope = cache_k_rope.shape[-1]
    pool_w = pool_scale.shape[1]
    v_head = w_uv.shape[3]
    n_mem, mem_heads, mem_hd = cache_mem_k.shape[2], cache_mem_k.shape[3], cache_mem_k.shape[4]
    n_pages = page_table.shape[1]
    assert cache_kv_latent.shape[2] == PAGE_SIZE
    past = n_pages * PAGE_SIZE
    mla_scale = float((nope + rope) ** -0.5)
    mem_scale = float(mem_hd ** -0.5)
    dims = (q_lora, kv_lora, pool_w, heads, nope, rope)
    l = 0

    i1, i2, i3 = q_lora, q_lora + kv_lora, q_lora + kv_lora + rope
    wi = w_in[l]
    w_kr = wi[:, i2:i3]
    win = jnp.concatenate([wi[:, :i2], wi[:, i3:], w_kr, _swap_halves(w_kr)], -1).astype(BF16)
    wq3 = w_uq[l].reshape(q_lora, heads, nope + rope)
    wq_rope = wq3[:, :, nope:]
    wuq = jnp.concatenate([wq3[:, :, :nope].reshape(q_lora, heads * nope),
                           wq_rope.reshape(q_lora, heads * rope),
                           _swap_halves(wq_rope).reshape(q_lora, heads * rope)], -1).astype(BF16)
    wuk = jnp.transpose(w_uk[l], (1, 2, 0)).astype(BF16)
    wuv = jnp.transpose(w_uv[l], (1, 0, 2)).astype(BF16)
    wpool = w_pool[l].astype(BF16)
    wout = w_out[l].astype(BF16)
    wxq, wxo = w_xq[l].astype(BF16), w_xo[l].astype(BF16)
    wmk, wmv = w_mk[l].astype(BF16), w_mv[l].astype(BF16)
    wg, wu, wd = w_gate[l].astype(BF16), w_up[l].astype(BF16), w_down[l].astype(BF16)
    gmix, gq, gkv = _row(norm_mix[l]), _row(q_norm[l]), _row(kv_norm[l])
    gx, gmem, gffn, gfin = _row(norm_x[l]), _row(mem_norm[l]), _row(norm_ffn[l]), _row(norm_final)
    pscale = _row(pool_scale[l])
    front_consts = (gmix, win, gq, wuq, gkv, wuk)
    hr = heads * rope

    tm = _tile(seq, 1024)
    cos_p, sin_p = _rope_tables(jnp.arange(seq), rope, heads)
    row_spec = lambda w: pl.BlockSpec((1, tm, w), lambda b, t: (b, t, 0))
    head_spec = lambda w: pl.BlockSpec((1, heads, tm, w), lambda b, t: (b, 0, t, 0))
    tab_spec = pl.BlockSpec((tm, hr), lambda b, t: (t, 0))
    col_spec = lambda w: pl.BlockSpec((1, w, tm), lambda b, t: (b, 0, t))
    qlat_p, qrope_p, c_p, krt_p, cbf_p, ctbf_p, krtbf_p, py_p, pst_p = pl.pallas_call(
        functools.partial(_front_prompt_kernel, scale=mla_scale, dims=dims, tm=tm),
        grid=(bsz, seq // tm),
        in_specs=[row_spec(d)] + [_const_spec(a.shape) for a in front_consts]
        + [tab_spec, tab_spec, _const_spec(wpool.shape), _const_spec(pscale.shape)],
        out_specs=[head_spec(kv_lora), head_spec(rope), row_spec(kv_lora), col_spec(rope),
                   row_spec(kv_lora), col_spec(kv_lora), col_spec(rope), row_spec(pool_w),
                   pl.BlockSpec((1, POOL_HALO, pool_w), lambda b, t: (b, 0, 0))],
        out_shape=[jax.ShapeDtypeStruct((bsz, heads, seq, kv_lora), BF16),
                   jax.ShapeDtypeStruct((bsz, heads, seq, rope), BF16),
                   jax.ShapeDtypeStruct((bsz, seq, kv_lora), F32),
                   jax.ShapeDtypeStruct((bsz, rope, seq), F32),
                   jax.ShapeDtypeStruct((bsz, seq, kv_lora), BF16),
                   jax.ShapeDtypeStruct((bsz, kv_lora, seq), BF16),
                   jax.ShapeDtypeStruct((bsz, rope, seq), BF16),
                   jax.ShapeDtypeStruct((bsz, seq, pool_w), BF16),
                   jax.ShapeDtypeStruct((bsz, POOL_HALO, pool_w), F32)],
        scratch_shapes=[pltpu.VMEM((POOL_HALO, pool_w), F32)],
        compiler_params=_params(2),
        name="front_prompt",
    )(x_prompt, *front_consts, cos_p, sin_p, wpool, pscale)

    xs = x_sample.reshape(db, d)
    cos_s, sin_s = _rope_tables(jnp.full((db,), past), rope, heads)
    prev_s = jnp.transpose(cache_pool[l], (1, 0, 2))
    sample_in = (xs, *front_consts, cos_s, sin_s, wpool, pscale, prev_s)
    qlat_s, qrope_s, c_s, kr_s, py_s, up_s = pl.pallas_call(
        functools.partial(_front_sample_kernel, scale=mla_scale, dims=dims, past=past),
        grid=(1,),
        in_specs=[_const_spec(a.shape) for a in sample_in],
        out_specs=[_whole_spec(s) for s in ((heads, db, kv_lora), (heads, db, rope), (db, kv_lora),
                                            (db, rope), (db, pool_w), (db, pool_w))],
        out_shape=[jax.ShapeDtypeStruct((heads, db, kv_lora), BF16),
                   jax.ShapeDtypeStruct((heads, db, rope), BF16),
                   jax.ShapeDtypeStruct((db, kv_lora), F32),
                   jax.ShapeDtypeStruct((db, rope), F32),
                   jax.ShapeDtypeStruct((db, pool_w), BF16),
                   jax.ShapeDtypeStruct((db, pool_w), F32)],
        compiler_params=_params(1),
        name="front_sample",
    )(*sample_in)

    x1_p, olat_s = _attention_with_paged(
        qlat_p, qrope_p, ctbf_p, krtbf_p, cbf_p, x_prompt, py_p, wuv, wout,
        page_table, jnp.swapaxes(qlat_s, 0, 1), jnp.swapaxes(qrope_s, 0, 1),
        c_s.reshape(db, 1, kv_lora), kr_s.reshape(db, 1, rope),
        cache_kv_latent[l], jnp.swapaxes(cache_k_rope[l], 1, 2),
        tq=_tile(seq, 512), n_host=db)

    assert mem_hd % LANES == 0
    slab = mem_heads * (mem_hd // LANES)

    def slab_view(a):
        n = a.shape[0]
        a = a.reshape(n, n_mem, mem_heads, mem_hd // LANES, LANES)
        return jnp.transpose(a, (0, 1, 3, 2, 4)).reshape(n, n_mem * slab, LANES)

    def from_slabs(a):
        n = a.shape[0]
        a = a.reshape(n, n_mem, mem_hd // LANES, mem_heads, LANES)
        return jnp.transpose(a, (0, 1, 3, 2, 4)).reshape(n, n_mem, mem_heads, mem_hd)

    tmem = _tile(seq, 1024)
    xm_spec = pl.BlockSpec((1, tmem, d), lambda b, i: (b, i, 0))
    mem_spec = pl.BlockSpec((1, n_mem, d), lambda b, i: (b, 0, 0))
    kv_out_spec = pl.BlockSpec((1, n_mem * slab, LANES), lambda b, i: (b, 0, 0))
    x2_p, mk_p, mv_p = pl.pallas_call(
        functools.partial(_mem_attn_prompt_kernel, heads=mem_heads, scale=mem_scale),
        grid=(bsz, seq // tmem),
        in_specs=[xm_spec, _const_spec(gx.shape), _const_spec(wxq.shape), mem_spec, _const_spec(gmem.shape),
                  _const_spec(wmk.shape), _const_spec(wmv.shape), _const_spec(wxo.shape)],
        out_specs=[xm_spec, kv_out_spec, kv_out_spec],
        out_shape=[jax.ShapeDtypeStruct((bsz, seq, d), F32)]
        + [jax.ShapeDtypeStruct((bsz, n_mem * slab, LANES), F32)] * 2,
        scratch_shapes=[pltpu.VMEM((n_mem, d), BF16), pltpu.VMEM((n_mem, d), BF16)],
        compiler_params=_params(2),
        name="mem_attn_prompt",
    )(x1_p, gx, wxq, mem_prompt, gmem, wmk, wmv, wxo)

    y_prompt, y_sample = _ffn_with_sample_tail(
        x2_p.reshape(bsz * seq, d), gffn, wg, wu, wd, gfin,
        jnp.swapaxes(olat_s, 0, 1), py_s, xs, wuv, wout,
        gx, wxq, slab_view(cache_mem_k[l]), slab_view(cache_mem_v[l]), wxo,
        tm=_tile(bsz * seq, 256), heads=mem_heads, scale=mem_scale, n_mem=n_mem)
    y_prompt = y_prompt.reshape(bsz, seq, d)
    y_sample = y_sample.reshape(db, ts, d)

    return (y_prompt, y_sample,
            c_p[None], jnp.swapaxes(krt_p, 1, 2)[None], pst_p[None, :, POOL_HALO - POOL_STATE:],
            from_slabs(mk_p)[None], from_slabs(mv_p)[None],
            c_s.reshape(depth, db, ts, kv_lora), kr_s.reshape(depth, db, ts, rope),
            jnp.concatenate([cache_pool[l][:, 1:], up_s[:, None]], 1)[None])
```

```python
import functools

import jax
import jax.numpy as jnp
import numpy as np
from jax import lax
from jax.experimental import pallas as pl
from jax.experimental.pallas import tpu as pltpu

F32 = jnp.float32
BF16 = jnp.bfloat16

EPS = 1e-6
ROPE_THETA = 10000.0
PAGE_SIZE = 128
POOL_WINDOWS = (2, 4, 8, 16)
POOL_STATE = max(POOL_WINDOWS) - 1
POOL_HALO = 16
LANES = 128
VMEM_LIMIT = 56 * 1024 * 1024
ATTN_ROWS = 256
ATTN_AHEAD = 3


def _rms(x, g):
    return x * lax.rsqrt(jnp.mean(x * x, -1, keepdims=True) + EPS) * g


def _dot(a, b):
    return jnp.dot(a, b, preferred_element_type=F32)


def _dot_nt(a, b):
    return lax.dot_general(a, b, (((1,), (1,)), ((), ())), preferred_element_type=F32)


def _lane_tile(x, n):
    return jnp.concatenate([x] * n, -1)


def _const_spec(shape):
    nd = len(shape)
    return pl.BlockSpec(shape, lambda *_: (0,) * nd, pipeline_mode=pl.Buffered(1))


def _whole_spec(shape):
    nd = len(shape)
    return pl.BlockSpec(shape, lambda *_: (0,) * nd)


def _params(n_axes):
    return pltpu.CompilerParams(dimension_semantics=("arbitrary",) * n_axes,
                                vmem_limit_bytes=VMEM_LIMIT)


def _mla_inputs(x, gmix, win, gq, wuq, gkv, wuk, cos, sin, scale, dims):
    q_lora, kv_lora, pool_w, heads, nope, rope = dims
    h = _rms(x, gmix).astype(BF16)
    u = _dot(h, win)
    i1, i2, i3 = q_lora, q_lora + kv_lora, q_lora + kv_lora + pool_w
    cq, ckv, up, krs = u[:, :i1], u[:, i1:i2], u[:, i2:i3], u[:, i3:]
    q = _dot(_rms(cq, gq).astype(BF16), wuq)
    n0 = heads * nope
    n1 = n0 + heads * rope
    q_rope = (q[:, n0:n1] * cos + q[:, n1:] * sin) * scale
    q_lat = [(_dot(q[:, hh * nope:(hh + 1) * nope].astype(BF16), wuk[hh]) * scale).astype(BF16)
             for hh in range(heads)]
    q_rope = [q_rope[:, hh * rope:(hh + 1) * rope].astype(BF16) for hh in range(heads)]
    c = _rms(ckv, gkv)
    k_rope = krs * cos[:, :2 * rope] + pltpu.roll(krs, rope, 1) * sin[:, :2 * rope]
    return q_lat, q_rope, c, k_rope, up


def _pool_project(sums, up, cnt, wpool, pscale):
    cg = wpool.shape[-1]
    ys = []
    for g in range(len(POOL_WINDOWS)):
        m = sums[g] / cnt[g] - up[:, g * cg:(g + 1) * cg]
        ys.append(_dot(m.astype(BF16), wpool[g]))
    return jnp.concatenate(ys, -1) * pscale


def _front_prompt_kernel(x_ref, gmix_ref, win_ref, gq_ref, wuq_ref, gkv_ref, wuk_ref, cos_ref, sin_ref,
                         wpool_ref, pscale_ref,
                         qlat_ref, qrope_ref, c_ref, krt_ref, cbf_ref, ctbf_ref, krtbf_ref, py_ref, pst_ref,
                         prev_ref, *, scale, dims, tm):
    t = pl.program_id(1)
    heads, rope = dims[3], dims[5]
    q_lat, q_rope, c, k_rope, up = _mla_inputs(
        x_ref[0], gmix_ref[...], win_ref[...], gq_ref[...], wuq_ref[...], gkv_ref[...], wuk_ref,
        cos_ref[...], sin_ref[...], scale, dims)
    for hh in range(heads):
        qlat_ref[0, hh] = q_lat[hh]
        qrope_ref[0, hh] = q_rope[hh]
    c_ref[0] = c
    cbf_ref[0] = c.astype(BF16)
    ctbf_ref[0] = c.T.astype(BF16)
    krt = k_rope.T[:rope]
    krt_ref[0] = krt
    krtbf_ref[0] = krt.astype(BF16)

    @pl.when(t == 0)
    def _():
        prev_ref[...] = jnp.zeros_like(prev_ref)

    cg = wpool_ref.shape[-1]
    e = jnp.concatenate([prev_ref[...], up], 0)
    tail = up[tm - POOL_HALO:, :]
    prev_ref[...] = tail
    pst_ref[0] = tail
    sums = []
    s = e
    for g, w in enumerate(POOL_WINDOWS):
        s = s[:, (cg if g else 0):]
        s = s + pltpu.roll(s, w // 2, 0)
        sums.append(s[POOL_HALO:, :cg])
    pos1 = t * tm + lax.broadcasted_iota(jnp.int32, (tm, cg), 0) + 1
    cnt = [jnp.minimum(w, pos1).astype(F32) for w in POOL_WINDOWS]
    py_ref[0] = _pool_project(sums, up, cnt, wpool_ref, pscale_ref[...]).astype(BF16)


def _front_sample_kernel(x_ref, gmix_ref, win_ref, gq_ref, wuq_ref, gkv_ref, wuk_ref, cos_ref, sin_ref,
                         wpool_ref, pscale_ref, prev_ref,
                         qlat_ref, qrope_ref, c_ref, kr_ref, py_ref, up_ref, *, scale, dims, past):
    heads, rope = dims[3], dims[5]
    q_lat, q_rope, c, k_rope, up = _mla_inputs(
        x_ref[...], gmix_ref[...], win_ref[...], gq_ref[...], wuq_ref[...], gkv_ref[...], wuk_ref,
        cos_ref[...], sin_ref[...], scale, dims)
    for hh in range(heads):
        qlat_ref[hh] = q_lat[hh]
        qrope_ref[hh] = q_rope[hh]
    c_ref[...] = c
    kr_ref[...] = k_rope[:, :rope]
    up_ref[...] = up

    cg = wpool_ref.shape[-1]
    sums, cnt = [], []
    for g, w in enumerate(POOL_WINDOWS):
        s = up[:, g * cg:(g + 1) * cg]
        for j in range(1, w):
            s = s + prev_ref[POOL_STATE - j][:, g * cg:(g + 1) * cg]
        sums.append(s)
        cnt.append(float(min(w, past + 1)))
    py_ref[...] = _pool_project(sums, up, cnt, wpool_ref, pscale_ref[...]).astype(BF16)


def _mix_out(o_lat_heads, py, x, wuv_ref, wout_ref):
    o_mla = jnp.concatenate(
        [_dot(o.astype(BF16), wuv_ref[hh]) for hh, o in enumerate(o_lat_heads)], -1).astype(BF16)
    wm = o_mla.shape[-1]
    return x + _dot(o_mla, wout_ref[:wm, :]) + _dot(py, wout_ref[wm:, :])


def _paged_attention(ql, qr, cn, krn, cbf, rot, tbuf, pieces):
    cnr = cn.astype(BF16).astype(F32)
    krnr = krn.astype(BF16).astype(F32)
    w = cbf.shape[0] // pieces

    def scores(k):
        tb = tbuf.at[k % 2]
        tb[...] = cbf[k * w:(k + 1) * w, :].T
        return (_dot(ql, tb[...])
                + _dot(qr, rot[:, k * w:(k + 1) * w].astype(BF16)))

    def softmax(s_parts):
        s = jnp.concatenate(s_parts, -1)
        s_new = (jnp.sum(ql.astype(F32) * cnr, -1, keepdims=True)
                 + jnp.sum(qr.astype(F32) * krnr, -1, keepdims=True))
        m = jnp.maximum(jnp.max(s, -1, keepdims=True), s_new)
        p = jnp.exp(s - m)
        p_new = jnp.exp(s_new - m)
        return p.astype(BF16), jnp.sum(p, -1, keepdims=True) + p_new, p_new

    def values(p, k):
        return _dot(p[:, k * w:(k + 1) * w], cbf[k * w:(k + 1) * w, :])

    def finish(v_parts, l, p_new):
        return (sum(v_parts) + p_new.astype(BF16).astype(F32) * cnr) / l

    return scores, softmax, values, finish


def _host_request(pt_ref, cache_c, cache_rt, kbuf, rbuf, cbf, tbuf, sem, n_pages,
                  req, first, last, prime, sql_ref, sqr_ref, cn_ref, krn_ref, pieces):
    def page_copies(rq, sl):
        lat, rot = [], []
        for p in range(n_pages):
            page = pt_ref[rq * n_pages + p]
            toks = pl.ds(p * PAGE_SIZE, PAGE_SIZE)
            lat.append(pltpu.make_async_copy(cache_c.at[page], kbuf.at[sl, toks, :], sem.at[0, sl]))
            rot.append(pltpu.make_async_copy(cache_rt.at[page], rbuf.at[sl, :, toks], sem.at[1, sl]))
        return lat + rot

    sl = (req - first) % 2

    @pl.when(prime)
    def _():
        for cp in page_copies(req, sl):
            cp.start()

    for cp in page_copies(req, sl):
        cp.wait()
    nxt = jnp.minimum(req + 1, last)
    for cp in page_copies(nxt, 1 - sl):
        cp.start()
    cbf[...] = kbuf[sl].astype(BF16)
    fns = _paged_attention(sql_ref[req], sqr_ref[req], cn_ref[req], krn_ref[req], cbf, rbuf.at[sl], tbuf,
                           pieces)

    def drain():
        @pl.when(req == last)
        def _():
            for cp in page_copies(nxt, 1 - sl):
                cp.wait()

    return fns, drain


def _attn_prompt_kernel(tab_ref, pt_ref,
                        ql_ref, qr_ref, kt_ref, rt_ref, v_ref, x_ref, py_ref, wuv_ref, wout_ref,
                        sql_ref, sqr_ref, cn_ref, krn_ref, cache_c, cache_rt,
                        o_ref, so_ref,
                        m_ref, l_ref, acc_ref, kbuf, rbuf, cbf, tbuf, sem,
                        *, tq, heads, n_pages, n_req, n_steps):
    step = pl.program_id(0)
    i = tab_ref[n_steps + step]
    j = tab_ref[2 * n_steps + step]
    req = tab_ref[3 * n_steps + step]
    tk = tq
    chains = [(hh, r) for hh in range(heads) for r in range(tq // ATTN_ROWS)]

    @pl.when(j == 0)
    def _():
        m_ref[...] = jnp.full_like(m_ref, -jnp.inf)
        l_ref[...] = jnp.zeros_like(l_ref)
        acc_ref[...] = jnp.zeros_like(acc_ref)

    def attend(diagonal, with_sample):
        if with_sample:
            pieces = len(chains) // 2
            (s_scores, s_softmax, s_values, s_finish), drain = _host_request(
                pt_ref, cache_c, cache_rt, kbuf, rbuf, cbf, tbuf, sem, n_pages,
                req, 0, n_req - 1, req == 0, sql_ref, sqr_ref, cn_ref, krn_ref, pieces)
            s_parts, v_parts = [], []

        off = pl.multiple_of(j * tk, tk)
        kt = kt_ref[0, :, pl.ds(off, tk)]
        rt = rt_ref[0, :, pl.ds(off, tk)]
        v = v_ref[0, pl.ds(off, tk), :]

        def width(r):
            return (r + 1) * ATTN_ROWS if diagonal else tk

        def scores(ci):
            hh, r = chains[ci]
            rows = pl.ds(r * ATTN_ROWS, ATTN_ROWS)
            w = width(r)
            return _dot(ql_ref[0, hh, rows, :], kt[:, :w]) + _dot(qr_ref[0, hh, rows, :], rt[:, :w])

        pending = [scores(ci) for ci in range(ATTN_AHEAD)]
        for ci, (hh, r) in enumerate(chains):
            if ci + ATTN_AHEAD < len(chains):
                pending.append(scores(ci + ATTN_AHEAD))
            if with_sample and ci < pieces:
                s_parts.append(s_scores(ci))
            if with_sample and ci == pieces:
                p_sample, l_sample, p_new = s_softmax(s_parts)
            if with_sample and ci >= pieces:
                v_parts.append(s_values(p_sample, ci - pieces))
            s = pending[ci]
            if diagonal:
                qpos = r * ATTN_ROWS + lax.broadcasted_iota(jnp.int32, s.shape, 0)
                kpos = lax.broadcasted_iota(jnp.int32, s.shape, 1)
                s = jnp.where(kpos <= qpos, s, -jnp.inf)
            m_prev = m_ref[ci]
            m_new = jnp.maximum(m_prev, jnp.max(s, -1, keepdims=True))
            alpha = jnp.exp(m_prev - m_new)
            p = jnp.exp(s - _lane_tile(m_new, width(r) // LANES))
            l_ref[ci] = alpha * l_ref[ci] + jnp.sum(p, -1, keepdims=True)
            acc_ref[ci] = (_lane_tile(alpha, acc_ref.shape[-1] // LANES) * acc_ref[ci]
                           + _dot(p.astype(BF16), v[:width(r)]))
            m_ref[ci] = m_new
        if with_sample:
            so_ref[req] = s_finish(v_parts, l_sample, p_new)
            drain()

    for diagonal in (False, True):
        for with_sample in (False, True):
            on_diag = (j == i) if diagonal else (j < i)
            hosting = (req >= 0) if with_sample else (req < 0)
            pl.when(on_diag & hosting)(functools.partial(attend, diagonal, with_sample))

    @pl.when(j == i)
    def _():
        nr = tq // ATTN_ROWS
        rep = acc_ref.shape[-1] // LANES
        o_heads = [jnp.concatenate([acc_ref[hh * nr + r] / _lane_tile(l_ref[hh * nr + r], rep)
                                    for r in range(nr)], 0)
                   for hh in range(heads)]
        o_ref[0] = _mix_out(o_heads, py_ref[0], x_ref[0], wuv_ref, wout_ref)


def _mem_attn_prompt_kernel(x_ref, g_ref, wq_ref, mem_ref, gm_ref, wk_ref, wv_ref, wo_ref,
                            o_ref, mk_ref, mv_ref, kbf, vbf, *, heads, scale):
    @pl.when(pl.program_id(1) == 0)
    def _():
        m = _rms(mem_ref[0], gm_ref[...]).astype(BF16)
        n_mem = m.shape[0]
        rr = mk_ref.shape[1] // n_mem
        for w_ref, out_ref, bf in ((wk_ref, mk_ref, kbf), (wv_ref, mv_ref, vbf)):
            kv = _dot(m, w_ref[...])
            bf[...] = kv.astype(BF16)
            for r in range(rr):
                j = _slab_col(r, heads, rr)
                out_ref[0, pl.ds(r, n_mem, stride=rr), :] = kv[:, j * LANES:(j + 1) * LANES]

    x = x_ref[0]
    q = (_dot(_rms(x, g_ref[...]).astype(BF16), wq_ref[...]) * scale).astype(BF16)
    hd = q.shape[-1] // heads
    outs = []
    for hh in range(heads):
        sl = slice(hh * hd, (hh + 1) * hd)
        s = _dot_nt(q[:, sl], kbf[:, sl])
        p = jnp.exp(s - jnp.max(s, -1, keepdims=True))
        o = _dot(p.astype(BF16), vbf[:, sl]) / jnp.sum(p, -1, keepdims=True)
        outs.append(o.astype(BF16))
    o_ref[0] = x + _dot(jnp.concatenate(outs, -1), wo_ref[...])


def _slab_col(r, heads, rr):
    return (r % heads) * (rr // heads) + r // heads


def _mem_query_slabs(x, g, wq_ref, q_ref, heads, scale):
    db = x.shape[0]
    rr = q_ref.shape[0] // db
    hq = _rms(x, g).astype(BF16)
    for r in range(rr):
        j = _slab_col(r, heads, rr)
        q_ref[pl.ds(r, db, stride=rr), :] = _dot(hq, wq_ref[:, j * LANES:(j + 1) * LANES]) * scale


def _mem_attend_request(q_ref, a_ref, k, v, request, heads, n_mem):
    rr = k.shape[0] // n_mem
    row = pl.multiple_of(request * rr, rr)
    qv = q_ref[pl.ds(row, rr), :]
    part = jnp.sum(k.reshape(n_mem, rr, LANES) * qv[None], -1, keepdims=True)
    s = part
    for d in range(1, rr // heads):
        s = s + pltpu.roll(part, d * heads, 1)
    p = jnp.exp(s - jnp.max(s, 0, keepdims=True))
    l = jnp.sum(p, 0)
    a_ref[pl.ds(row, rr), :] = jnp.sum(p * v.reshape(n_mem, rr, LANES), 0) / l


def _mem_output(x, a_ref, wo_ref, heads):
    db = x.shape[0]
    rr = a_ref.shape[0] // db
    acc = x
    for r in range(rr):
        j = _slab_col(r, heads, rr)
        a = a_ref[pl.ds(r, db, stride=rr), :].astype(BF16)
        acc = acc + _dot(a, wo_ref[j * LANES:(j + 1) * LANES, :])
    return acc


def _swiglu_final(x, g, wg_ref, wu_ref, wd_ref, gf):
    h = _rms(x, g).astype(BF16)
    gate = _dot(h, wg_ref[...])
    a = (gate * jax.nn.sigmoid(gate) * _dot(h, wu_ref[...])).astype(BF16)
    return _rms(x + _dot(a, wd_ref[...]), gf)


def _ffn_mem_kernel(x_ref, g_ref, wg_ref, wu_ref, wd_ref, gf_ref,
                    ol_ref, py_ref, xs_ref, wuv_ref, wout_ref, gx_ref, wq_ref, k_ref, v_ref, wo_ref,
                    y_ref, ys_ref, x1_ref, q_ref, a_ref, *, mla_heads, heads, scale, nb, n_mem):
    s_id = pl.program_id(0)

    @pl.when(s_id == 0)
    def _():
        x1 = _mix_out([ol_ref[hh] for hh in range(mla_heads)], py_ref[...], xs_ref[...], wuv_ref, wout_ref)
        x1_ref[...] = x1
        _mem_query_slabs(x1, gx_ref[...], wq_ref, q_ref, heads, scale)

    y_ref[...] = _swiglu_final(x_ref[...], g_ref[...], wg_ref, wu_ref, wd_ref, gf_ref[...])
    for bb in range(nb):
        _mem_attend_request(q_ref, a_ref, k_ref[bb], v_ref[bb], s_id * nb + bb, heads, n_mem)

    @pl.when(s_id == pl.num_programs(0) - 1)
    def _():
        x2 = _mem_output(x1_ref[...], a_ref, wo_ref, heads)
        ys_ref[...] = _swiglu_final(x2, g_ref[...], wg_ref, wu_ref, wd_ref, gf_ref[...])


def _rope_tables(pos, rope, heads):
    half = rope // 2
    inv = jnp.power(ROPE_THETA, -(jnp.arange(half, dtype=F32) / half))
    ang = pos.astype(F32)[:, None] * inv[None, :]
    cos, sin = jnp.cos(ang), jnp.sin(ang)
    return (jnp.tile(jnp.concatenate([cos, cos], -1), (1, heads)),
            jnp.tile(jnp.concatenate([-sin, sin], -1), (1, heads)))


def _swap_halves(w):
    half = w.shape[-1] // 2
    return jnp.concatenate([w[..., half:], w[..., :half]], -1)


def _row(v):
    return v.reshape(1, -1)


def _tile(n, pref):
    return pref if n % pref == 0 else n


def _attention_schedule(bsz, n_tiles, n_req):
    steps = [(b, i, j) for b in range(bsz) for i in range(n_tiles) for j in range(i + 1)]
    n_steps = len(steps)
    assert n_req <= n_steps, "at most one sample request per attention step"
    tab = np.zeros((4, n_steps), np.int32)
    for s, (b, i, j) in enumerate(steps):
        r = s * n_req // n_steps
        hosted = (s + 1) * n_req // n_steps > r
        tab[:, s] = (b, i, j, r if hosted else -1)
    return tab


def _attention_with_paged(qlat_p, qrope_p, ctbf, krtbf, cbf, x, py, wuv, wout,
                          page_table, qlat_s, qrope_s, c_new, kr_new, cache_c, cache_rt, tq, n_host):
    bsz, heads, seq, kv = qlat_p.shape
    rope = qrope_p.shape[-1]
    d = x.shape[-1]
    n_pages = page_table.shape[1]
    past = n_pages * PAGE_SIZE
    assert seq % tq == 0 and tq % ATTN_ROWS == 0
    n_chain = heads * (tq // ATTN_ROWS)
    tab = _attention_schedule(bsz, seq // tq, n_host)
    n_steps = tab.shape[1]
    col = lambda r: (lambda s, tab_ref, pt: tab_ref[r * n_steps + s])
    bb, ii = col(0), col(1)
    q_spec = lambda w: pl.BlockSpec((1, heads, tq, w), lambda *a: (bb(*a), 0, ii(*a), 0))
    batch_spec = lambda r, c: pl.BlockSpec((1, r, c), lambda *a: (bb(*a), 0, 0), pipeline_mode=pl.Buffered(1))
    row_spec = lambda w: pl.BlockSpec((1, tq, w), lambda *a: (bb(*a), ii(*a), 0))
    return pl.pallas_call(
        functools.partial(_attn_prompt_kernel, tq=tq, heads=heads, n_pages=n_pages, n_req=n_host,
                          n_steps=n_steps),
        grid_spec=pltpu.PrefetchScalarGridSpec(
            num_scalar_prefetch=2,
            grid=(n_steps,),
            in_specs=[q_spec(kv), q_spec(rope), batch_spec(kv, seq), batch_spec(rope, seq), batch_spec(seq, kv),
                      row_spec(d), row_spec(py.shape[-1]), _const_spec(wuv.shape), _const_spec(wout.shape),
                      _const_spec(qlat_s.shape), _const_spec(qrope_s.shape), _const_spec(c_new.shape),
                      _const_spec(kr_new.shape),
                      pl.BlockSpec(memory_space=pl.ANY), pl.BlockSpec(memory_space=pl.ANY)],
            out_specs=[row_spec(d), _whole_spec((n_host, heads, kv))],
            scratch_shapes=[pltpu.VMEM((n_chain, ATTN_ROWS, LANES), F32),
                            pltpu.VMEM((n_chain, ATTN_ROWS, LANES), F32),
                            pltpu.VMEM((n_chain, ATTN_ROWS, kv), F32),
                            pltpu.VMEM((2, past, kv), F32), pltpu.VMEM((2, rope, past), F32),
                            pltpu.VMEM((past, kv), BF16), pltpu.VMEM((2, kv, past // (n_chain // 2)), BF16),
                            pltpu.SemaphoreType.DMA((2, 2))],
        ),
        out_shape=[jax.ShapeDtypeStruct((bsz, seq, d), F32), jax.ShapeDtypeStruct((n_host, heads, kv), F32)],
        compiler_params=_params(1),
        name="attn_prompt_paged",
    )(jnp.asarray(tab.reshape(-1)), page_table.reshape(-1),
      qlat_p, qrope_p, ctbf, krtbf, cbf, x, py, wuv, wout,
      qlat_s, qrope_s, c_new, kr_new, cache_c, cache_rt)


def _ffn_with_sample_tail(x, g, wg, wu, wd, gf, olat, py, xs, wuv, wout, gx, wxq, mem_k, mem_v, wxo,
                          tm, heads, scale, n_mem):
    rows, d = x.shape
    db = xs.shape[0]
    n_steps = rows // tm
    assert rows % tm == 0 and db % n_steps == 0
    nb = db // n_steps
    slab = mem_k.shape[1] // n_mem
    x_spec = pl.BlockSpec((tm, d), lambda s: (s, 0))
    cache_spec = pl.BlockSpec((nb,) + mem_k.shape[1:], lambda s: (s, 0, 0))
    resident = (g, wg, wu, wd, gf, olat, py, xs, wuv, wout, gx, wxq)
    return pl.pallas_call(
        functools.partial(_ffn_mem_kernel, mla_heads=olat.shape[0], heads=heads, scale=scale, nb=nb,
                          n_mem=n_mem),
        grid=(n_steps,),
        in_specs=[x_spec] + [_const_spec(a.shape) for a in resident]
        + [cache_spec, cache_spec, _const_spec(wxo.shape)],
        out_specs=[x_spec, _whole_spec((db, d))],
        out_shape=[jax.ShapeDtypeStruct((rows, d), F32), jax.ShapeDtypeStruct((db, d), F32)],
        scratch_shapes=[pltpu.VMEM((db, d), F32), pltpu.VMEM((db * slab, LANES), F32),
                        pltpu.VMEM((db * slab, LANES), F32)],
        compiler_params=_params(1),
        name="ffn_mem",
    )(x, *resident, mem_k, mem_v, wxo)


def kernel(x_prompt, x_sample, mem_prompt, cache_kv_latent, cache_k_rope, cache_pool, cache_mem_k, cache_mem_v, page_table, norm_mix, w_in, q_norm, w_uq, kv_norm, w_uk, w_uv, w_pool, pool_scale, w_out, norm_x, mem_norm, w_xq, w_mk, w_mv, w_xo, norm_ffn, w_gate, w_up, w_down, norm_final):
    depth = w_in.shape[0]
    assert depth == 1, "single-layer trunk"
    bsz, seq, d = x_prompt.shape
    db, ts, _ = x_sample.shape
    assert ts == 1, "one new token per request"
    q_lora, kv_lora = q_norm.shape[1], kv_norm.shape[1]
    heads, nope = w_uk.shape[2], w_uk.shape[3]
    rope = cache_k_rope.shape[-1]
    pool_w = pool_scale.shape[1]
    v_head = w_uv.shape[3]
    n_mem, mem_heads, mem_hd = cache_mem_k.shape[2], cache_mem_k.shape[3], cache_mem_k.shape[4]
    n_pages = page_table.shape[1]
    assert cache_kv_latent.shape[2] == PAGE_SIZE
    past = n_pages * PAGE_SIZE
    mla_scale = float((nope + rope) ** -0.5)
    mem_scale = float(mem_hd ** -0.5)
    dims = (q_lora, kv_lora, pool_w, heads, nope, rope)
    l = 0

    i1, i2, i3 = q_lora, q_lora + kv_lora, q_lora + kv_lora + rope
    wi = w_in[l]
    w_kr = wi[:, i2:i3]
    win = jnp.concatenate([wi[:, :i2], wi[:, i3:], w_kr, _swap_halves(w_kr)], -1).astype(BF16)
    wq3 = w_uq[l].reshape(q_lora, heads, nope + rope)
    wq_rope = wq3[:, :, nope:]
    wuq = jnp.concatenate([wq3[:, :, :nope].reshape(q_lora, heads * nope),
                           wq_rope.reshape(q_lora, heads * rope),
                           _swap_halves(wq_rope).reshape(q_lora, heads * rope)], -1).astype(BF16)
    wuk = jnp.transpose(w_uk[l], (1, 2, 0)).astype(BF16)
    wuv = jnp.transpose(w_uv[l], (1, 0, 2)).astype(BF16)
    wpool = w_pool[l].astype(BF16)
    wout = w_out[l].astype(BF16)
    wxq, wxo = w_xq[l].astype(BF16), w_xo[l].astype(BF16)
    wmk, wmv = w_mk[l].astype(BF16), w_mv[l].astype(BF16)
    wg, wu, wd = w_gate[l].astype(BF16), w_up[l].astype(BF16), w_down[l].astype(BF16)
    gmix, gq, gkv = _row(norm_mix[l]), _row(q_norm[l]), _row(kv_norm[l])
    gx, gmem, gffn, gfin = _row(norm_x[l]), _row(mem_norm[l]), _row(norm_ffn[l]), _row(norm_final)
    pscale = _row(pool_scale[l])
    front_consts = (gmix, win, gq, wuq, gkv, wuk)
    hr = heads * rope

    tm = _tile(seq, 1024)
    cos_p, sin_p = _rope_tables(jnp.arange(seq), rope, heads)
    row_spec = lambda w: pl.BlockSpec((1, tm, w), lambda b, t: (b, t, 0))
    head_spec = lambda w: pl.BlockSpec((1, heads, tm, w), lambda b, t: (b, 0, t, 0))
    tab_spec = pl.BlockSpec((tm, hr), lambda b, t: (t, 0))
    col_spec = lambda w: pl.BlockSpec((1, w, tm), lambda b, t: (b, 0, t))
    qlat_p, qrope_p, c_p, krt_p, cbf_p, ctbf_p, krtbf_p, py_p, pst_p = pl.pallas_call(
        functools.partial(_front_prompt_kernel, scale=mla_scale, dims=dims, tm=tm),
        grid=(bsz, seq // tm),
        in_specs=[row_spec(d)] + [_const_spec(a.shape) for a in front_consts]
        + [tab_spec, tab_spec, _const_spec(wpool.shape), _const_spec(pscale.shape)],
        out_specs=[head_spec(kv_lora), head_spec(rope), row_spec(kv_lora), col_spec(rope),
                   row_spec(kv_lora), col_spec(kv_lora), col_spec(rope), row_spec(pool_w),
                   pl.BlockSpec((1, POOL_HALO, pool_w), lambda b, t: (b, 0, 0))],
        out_shape=[jax.ShapeDtypeStruct((bsz, heads, seq, kv_lora), BF16),
                   jax.ShapeDtypeStruct((bsz, heads, seq, rope), BF16),
                   jax.ShapeDtypeStruct((bsz, seq, kv_lora), F32),
                   jax.ShapeDtypeStruct((bsz, rope, seq), F32),
                   jax.ShapeDtypeStruct((bsz, seq, kv_lora), BF16),
                   jax.ShapeDtypeStruct((bsz, kv_lora, seq), BF16),
                   jax.ShapeDtypeStruct((bsz, rope, seq), BF16),
                   jax.ShapeDtypeStruct((bsz, seq, pool_w), BF16),
                   jax.ShapeDtypeStruct((bsz, POOL_HALO, pool_w), F32)],
        scratch_shapes=[pltpu.VMEM((POOL_HALO, pool_w), F32)],
        compiler_params=_params(2),
        name="front_prompt",
    )(x_prompt, *front_consts, cos_p, sin_p, wpool, pscale)

    xs = x_sample.reshape(db, d)
    cos_s, sin_s = _rope_tables(jnp.full((db,), past), rope, heads)
    prev_s = jnp.transpose(cache_pool[l], (1, 0, 2))
    sample_in = (xs, *front_consts, cos_s, sin_s, wpool, pscale, prev_s)
    qlat_s, qrope_s, c_s, kr_s, py_s, up_s = pl.pallas_call(
        functools.partial(_front_sample_kernel, scale=mla_scale, dims=dims, past=past),
        grid=(1,),
        in_specs=[_const_spec(a.shape) for a in sample_in],
        out_specs=[_whole_spec(s) for s in ((heads, db, kv_lora), (heads, db, rope), (db, kv_lora),
                                            (db, rope), (db, pool_w), (db, pool_w))],
        out_shape=[jax.ShapeDtypeStruct((heads, db, kv_lora), BF16),
                   jax.ShapeDtypeStruct((heads, db, rope), BF16),
                   jax.ShapeDtypeStruct((db, kv_lora), F32),
                   jax.ShapeDtypeStruct((db, rope), F32),
                   jax.ShapeDtypeStruct((db, pool_w), BF16),
                   jax.ShapeDtypeStruct((db, pool_w), F32)],
        compiler_params=_params(1),
        name="front_sample",
    )(*sample_in)

    x1_p, olat_s = _attention_with_paged(
        qlat_p, qrope_p, ctbf_p, krtbf_p, cbf_p, x_prompt, py_p, wuv, wout,
        page_table, jnp.swapaxes(qlat_s, 0, 1), jnp.swapaxes(qrope_s, 0, 1),
        c_s.reshape(db, 1, kv_lora), kr_s.reshape(db, 1, rope),
        cache_kv_latent[l], jnp.swapaxes(cache_k_rope[l], 1, 2),
        tq=_tile(seq, 512), n_host=db)

    assert mem_hd % LANES == 0
    slab = mem_heads * (mem_hd // LANES)

    def slab_view(a):
        n = a.shape[0]
        a = a.reshape(n, n_mem, mem_heads, mem_hd // LANES, LANES)
        return jnp.transpose(a, (0, 1, 3, 2, 4)).reshape(n, n_mem * slab, LANES)

    def from_slabs(a):
        n = a.shape[0]
        a = a.reshape(n, n_mem, mem_hd // LANES, mem_heads, LANES)
        return jnp.transpose(a, (0, 1, 3, 2, 4)).reshape(n, n_mem, mem_heads, mem_hd)

    tmem = _tile(seq, 1024)
    xm_spec = pl.BlockSpec((1, tmem, d), lambda b, i: (b, i, 0))
    mem_spec = pl.BlockSpec((1, n_mem, d), lambda b, i: (b, 0, 0))
    kv_out_spec = pl.BlockSpec((1, n_mem * slab, LANES), lambda b, i: (b, 0, 0))
    x2_p, mk_p, mv_p = pl.pallas_call(
        functools.partial(_mem_attn_prompt_kernel, heads=mem_heads, scale=mem_scale),
        grid=(bsz, seq // tmem),
        in_specs=[xm_spec, _const_spec(gx.shape), _const_spec(wxq.shape), mem_spec, _const_spec(gmem.shape),
                  _const_spec(wmk.shape), _const_spec(wmv.shape), _const_spec(wxo.shape)],
        out_specs=[xm_spec, kv_out_spec, kv_out_spec],
        out_shape=[jax.ShapeDtypeStruct((bsz, seq, d), F32)]
        + [jax.ShapeDtypeStruct((bsz, n_mem * slab, LANES), F32)] * 2,
        scratch_shapes=[pltpu.VMEM((n_mem, d), BF16), pltpu.VMEM((n_mem, d), BF16)],
        compiler_params=_params(2),
        name="mem_attn_prompt",
    )(x1_p, gx, wxq, mem_prompt, gmem, wmk, wmv, wxo)

    y_prompt, y_sample = _ffn_with_sample_tail(
        x2_p.reshape(bsz * seq, d), gffn, wg, wu, wd, gfin,
        jnp.swapaxes(olat_s, 0, 1), py_s, xs, wuv, wout,
        gx, wxq, slab_view(cache_mem_k[l]), slab_view(cache_mem_v[l]), wxo,
        tm=_tile(bsz * seq, 256), heads=mem_heads, scale=mem_scale, n_mem=n_mem)
    y_prompt = y_prompt.reshape(bsz, seq, d)
    y_sample = y_sample.reshape(db, ts, d)

    return (y_prompt, y_sample,
            c_p[None], jnp.swapaxes(krt_p, 1, 2)[None], pst_p[None, :, POOL_HALO - POOL_STATE:],
            from_slabs(mk_p)[None], from_slabs(mv_p)[None],
            c_s.reshape(depth, db, ts, kv_lora), kr_s.reshape(depth, db, ts, rope),
            jnp.concatenate([cache_pool[l][:, 1:], up_s[:, None]], 1)[None])
```
